```python
import jax, jax.numpy as jnp
from jax import lax
import numpy as np

D_MODEL = 2048
BATCH = 8
SEQ = 2048
DEPTH = 4

N_MIXERS = 2
N_HEADS = 16
HEAD_DIM = D_MODEL // N_HEADS
DILATED_BRANCHES = ((128, 1), (512, 4), (2048, 16))
BLOCK = 128
KV_LATENT = 256
IDX_HEADS = 16
IDX_DIM = 128
TOPK_MAX = 256
D_FF = -(-8 * D_MODEL // (3 * 256)) * 256
D_IN_A = 3 * N_HEADS * HEAD_DIM
D_IN_B = N_HEADS * HEAD_DIM + KV_LATENT + IDX_HEADS * IDX_DIM + IDX_DIM + IDX_HEADS
EPS = 1e-6
NEG_BIG = -1e30

kernel_name = 'hybrid_dilated_dsa_decoder'


def rms_norm(x, gain):
    xf = x.astype(jnp.float32)
    y = xf * lax.rsqrt(jnp.mean(xf * xf, axis=-1, keepdims=True) + EPS)
    return (y * gain.astype(jnp.float32)).astype(x.dtype)


def alibi_slopes(n_heads):
    return jnp.exp2(-8.0 * jnp.arange(1, n_heads + 1, dtype=jnp.float32) / n_heads)


def adaln_params(c, w_ada, b_ada):
    mod = jax.nn.silu(c) @ w_ada + b_ada
    return [m[:, None, :] for m in jnp.split(mod, 6, axis=-1)]


def modulate(x, gain, shift, scale):
    return rms_norm(x, gain) * (1 + scale) + shift


def swiglu(h, w_gate_up, w_down):
    gate, up = jnp.split(h @ w_gate_up, 2, axis=-1)
    return (jax.nn.silu(gate) * up) @ w_down


def dilated_branch(q, k, v, slopes, window, dilation):
    b, s, h, hd = q.shape
    span = window // dilation
    n = s // dilation
    n_pad = -(-n // BLOCK) * BLOCK
    nb = n_pad // BLOCK

    def to_blocks(t):
        t = t.reshape(b, n, dilation, h, hd)
        t = jnp.pad(t, ((0, 0), (0, n_pad - n), (0, 0), (0, 0), (0, 0)))
        return t.reshape(b, nb, BLOCK, dilation, h, hd)

    def with_prev(t):
        prev = jnp.pad(t[:, :-1], ((0, 0), (1, 0), (0, 0), (0, 0), (0, 0), (0, 0)))
        return jnp.concatenate([prev, t], axis=2)

    qb = to_blocks(q)
    kb = with_prev(to_blocks(k))
    vb = with_prev(to_blocks(v))
    k_off = jnp.arange(2 * BLOCK)
    rel = jnp.arange(BLOCK)[:, None] + BLOCK - k_off[None, :]
    key_sub = jnp.arange(nb)[:, None] * BLOCK - BLOCK + k_off[None, :]
    valid = ((rel >= 0) & (rel <= span))[None] & (key_sub >= 0)[:, None, :]
    bias = -slopes[:, None, None] * (rel * dilation).astype(jnp.float32)[None]
    scores = jnp.einsum('bnqrhd,bnkrhd->bnrhqk', qb, kb,
                        preferred_element_type=jnp.float32) * hd ** -0.5 + bias
    scores = jnp.where(valid[None, :, None, None], scores, -jnp.inf)
    m = jnp.max(scores, axis=-1, keepdims=True)
    p = jnp.exp(scores - m)
    den = jnp.sum(p, axis=-1, keepdims=True)
    o = jnp.einsum('bnrhqk,bnkrhd->bnqrhd', p / den, vb.astype(jnp.float32))
    lse = jnp.transpose((m + jnp.log(den))[..., 0], (0, 1, 4, 2, 3))
    o = o.reshape(b, n_pad, dilation, h, hd)[:, :n].reshape(b, s, h, hd)
    lse = lse.reshape(b, n_pad, dilation, h)[:, :n].reshape(b, s, h)
    return o, lse


def dilated_attention(h, w_in, w_out, slopes):
    b, s, _ = h.shape
    qkv = (h @ w_in).reshape(b, s, 3, N_HEADS, HEAD_DIM)
    q, k, v = qkv[:, :, 0], qkv[:, :, 1], qkv[:, :, 2]
    outs, lses = [], []
    for window, dilation in DILATED_BRANCHES:
        o, lse = dilated_branch(q, k, v, slopes, window, dilation)
        outs.append(o)
        lses.append(lse)
    alpha = jax.nn.softmax(jnp.stack(lses), axis=0)
    o = jnp.einsum('gbsh,gbshd->bshd', alpha, jnp.stack(outs))
    return o.reshape(b, s, N_HEADS * HEAD_DIM).astype(h.dtype) @ w_out


def dsa_attention(h, w_in, kv_norm, w_uk, w_uv, w_out, slopes, topk):
    b, s, _ = h.shape
    nb = s // BLOCK
    sizes = (N_HEADS * HEAD_DIM, KV_LATENT, IDX_HEADS * IDX_DIM, IDX_DIM)
    q, c_kv, q_idx, k_idx, w_idx = jnp.split(h @ w_in, [int(o) for o in np.cumsum(sizes)], axis=-1)
    q = q.reshape(b, s, N_HEADS, HEAD_DIM)
    c_kv = rms_norm(c_kv, kv_norm)
    q_idx = q_idx.reshape(b, s, IDX_HEADS, IDX_DIM)
    w_idx = w_idx * IDX_HEADS ** -0.5
    q_lat = jnp.einsum('bshd,hdc->bshc', q, w_uk)
    key_pos = jnp.arange(s)

    def blocks(t):
        return jnp.moveaxis(t.reshape(b, nb, BLOCK, *t.shape[2:]), 1, 0)

    def one_block(args):
        blk, ql, qi, wi = args
        q_pos = blk * BLOCK + jnp.arange(BLOCK)
        logits = jnp.einsum('bqhd,bsd->bqhs', qi, k_idx,
                            preferred_element_type=jnp.float32) * IDX_DIM ** -0.5
        index_score = jnp.einsum('bqh,bqhs->bqs', wi.astype(jnp.float32), jax.nn.relu(logits))
        causal = key_pos[None, :] <= q_pos[:, None]
        index_score = jnp.where(causal[None], index_score, NEG_BIG)
        _, sel = lax.top_k(index_score, topk)
        kv_sel = jax.vmap(lambda kv, ix: kv[ix])(c_kv, sel)
        scores = jnp.einsum('bqhc,bqkc->bqhk', ql, kv_sel,
                            preferred_element_type=jnp.float32) * HEAD_DIM ** -0.5
        dist = (q_pos[None, :, None] - sel).astype(jnp.float32)
        scores = scores - slopes[None, None, :, None] * dist[:, :, None, :]
        scores = jnp.where((dist >= 0)[:, :, None, :], scores, -jnp.inf)
        probs = jax.nn.softmax(scores, axis=-1)
        return jnp.einsum('bqhk,bqkc->bqhc', probs, kv_sel.astype(jnp.float32))

    o_lat = lax.map(one_block, (jnp.arange(nb), blocks(q_lat), blocks(q_idx), blocks(w_idx)))
    o_lat = jnp.moveaxis(o_lat, 0, 1).reshape(b, s, N_HEADS, KV_LATENT)
    o = jnp.einsum('bshc,hcd->bshd', o_lat, w_uv.astype(jnp.float32))
    return o.reshape(b, s, N_HEADS * HEAD_DIM).astype(h.dtype) @ w_out


def setup_inputs(seed: int = 0) -> dict:
    key = jax.random.key(seed)
    keys = jax.random.split(key, 128)
    counter = [0]

    def normal(shape, scale):
        k = keys[counter[0]]
        counter[0] += 1
        return jax.random.normal(k, shape, jnp.float32) * scale

    def gain(n):
        return 1.0 + normal((n,), 0.02)

    inputs = {'x': normal((BATCH, SEQ, D_MODEL), 1.0), 'c': normal((BATCH, D_MODEL), 1.0)}
    for i in range(DEPTH):
        p = 'l%d_' % i
        inputs[p + 'norm_attn'] = gain(D_MODEL)
        inputs[p + 'ada_w'] = normal((D_MODEL, 6 * D_MODEL), 0.5 * D_MODEL ** -0.5)
        inputs[p + 'ada_b'] = normal((6 * D_MODEL,), 0.02)
        if i % N_MIXERS == 0:
            inputs[p + 'w_in'] = normal((D_MODEL, D_IN_A), D_MODEL ** -0.5)
        else:
            inputs[p + 'w_in'] = normal((D_MODEL, D_IN_B), D_MODEL ** -0.5)
            inputs[p + 'kv_norm'] = gain(KV_LATENT)
            inputs[p + 'w_uk'] = normal((N_HEADS, HEAD_DIM, KV_LATENT), HEAD_DIM ** -0.5)
            inputs[p + 'w_uv'] = normal((N_HEADS, KV_LATENT, HEAD_DIM), KV_LATENT ** -0.5)
        inputs[p + 'w_out'] = normal((N_HEADS * HEAD_DIM, D_MODEL), (N_HEADS * HEAD_DIM) ** -0.5)
        inputs[p + 'norm_ffn'] = gain(D_MODEL)
        inputs[p + 'w_gate_up'] = normal((D_MODEL, 2 * D_FF), D_MODEL ** -0.5)
        inputs[p + 'w_down'] = normal((D_FF, D_MODEL), D_FF ** -0.5)
    inputs['final_norm'] = gain(D_MODEL)
    return inputs


def reference(x, c,
              l0_norm_attn, l0_ada_w, l0_ada_b, l0_w_in, l0_w_out, l0_norm_ffn, l0_w_gate_up, l0_w_down,
              l1_norm_attn, l1_ada_w, l1_ada_b, l1_w_in, l1_kv_norm, l1_w_uk, l1_w_uv, l1_w_out,
              l1_norm_ffn, l1_w_gate_up, l1_w_down,
              l2_norm_attn, l2_ada_w, l2_ada_b, l2_w_in, l2_w_out, l2_norm_ffn, l2_w_gate_up, l2_w_down,
              l3_norm_attn, l3_ada_w, l3_ada_b, l3_w_in, l3_kv_norm, l3_w_uk, l3_w_uv, l3_w_out,
              l3_norm_ffn, l3_w_gate_up, l3_w_down,
              final_norm):
    slopes = alibi_slopes(N_HEADS)
    topk = min(TOPK_MAX, x.shape[1] // 4)
    layers = (
        (l0_norm_attn, l0_ada_w, l0_ada_b, (l0_w_in, l0_w_out), l0_norm_ffn, l0_w_gate_up, l0_w_down),
        (l1_norm_attn, l1_ada_w, l1_ada_b, (l1_w_in, l1_kv_norm, l1_w_uk, l1_w_uv, l1_w_out),
         l1_norm_ffn, l1_w_gate_up, l1_w_down),
        (l2_norm_attn, l2_ada_w, l2_ada_b, (l2_w_in, l2_w_out), l2_norm_ffn, l2_w_gate_up, l2_w_down),
        (l3_norm_attn, l3_ada_w, l3_ada_b, (l3_w_in, l3_kv_norm, l3_w_uk, l3_w_uv, l3_w_out),
         l3_norm_ffn, l3_w_gate_up, l3_w_down),
    )
    for i in range(DEPTH):
        norm_attn, ada_w, ada_b, mixer_w, norm_ffn, w_gate_up, w_down = layers[i]
        shift_a, scale_a, gate_a, shift_f, scale_f, gate_f = adaln_params(c, ada_w, ada_b)
        h = modulate(x, norm_attn, shift_a, scale_a)
        if i % N_MIXERS == 0:
            y = dilated_attention(h, *mixer_w, slopes)
        else:
            y = dsa_attention(h, *mixer_w, slopes, topk)
        x = x + gate_a * y
        h = modulate(x, norm_ffn, shift_f, scale_f)
        x = x + gate_f * swiglu(h, w_gate_up, w_down)
    return rms_norm(x, final_norm)
```

```python
import functools

import jax
import jax.numpy as jnp
from jax import lax
from jax.experimental import pallas as pl
from jax.experimental.pallas import tpu as pltpu

N_HEADS = 16
HEAD_DIM = 128
DILATED_BRANCHES = ((128, 1), (512, 4), (2048, 16))
BLOCK = 128
KV_LATENT = 256
IDX_HEADS = 16
IDX_DIM = 128
TOPK_MAX = 256
EPS = 1e-6
NEG_BIG = -1e30
MASK_DIST = 1e12

LANE = 128
VMEM_LIMIT = 48 * 1024 * 1024

F32 = jnp.float32
BF16 = jnp.bfloat16

_DSA_COLS = {
    "q": 0,
    "q_idx": N_HEADS * HEAD_DIM,
    "c_kv": N_HEADS * HEAD_DIM + IDX_HEADS * IDX_DIM,
    "k_idx": N_HEADS * HEAD_DIM + IDX_HEADS * IDX_DIM + KV_LATENT,
    "w_idx": N_HEADS * HEAD_DIM + IDX_HEADS * IDX_DIM + KV_LATENT + IDX_DIM,
}
_DSA_WIDTH = -(-(_DSA_COLS["w_idx"] + IDX_HEADS) // 512) * 512


def _params(*sem):
    return pltpu.CompilerParams(dimension_semantics=sem, vmem_limit_bytes=VMEM_LIMIT)


def _adaln_kernel(c_ref, w_ref, b_ref, o_ref):
    c = c_ref[...]
    sc = (c * jax.nn.sigmoid(c)).astype(BF16)
    acc = jnp.dot(sc, w_ref[...].astype(BF16), preferred_element_type=F32)
    o_ref[...] = acc + b_ref[...]


def _adaln(c, w, b, tn=1024):
    bsz, d = c.shape
    n = w.shape[1]
    return pl.pallas_call(
        _adaln_kernel,
        grid=(n // tn,),
        in_specs=[
            pl.BlockSpec((bsz, d), lambda j: (0, 0)),
            pl.BlockSpec((d, tn), lambda j: (0, j)),
            pl.BlockSpec((1, tn), lambda j: (0, j)),
        ],
        out_specs=pl.BlockSpec((bsz, tn), lambda j: (0, j)),
        out_shape=jax.ShapeDtypeStruct((bsz, n), F32),
        compiler_params=_params("parallel"),
        name="adaln",
    )(c, w, b.reshape(1, n))


def _modulate_kernel(x_ref, g_ref, shift_ref, scale_ref, o_ref):
    x = x_ref[...]
    ms = jnp.mean(x * x, axis=-1, keepdims=True)
    y = x * lax.rsqrt(ms + EPS) * g_ref[...]
    o_ref[...] = (y * (1.0 + scale_ref[0]) + shift_ref[0]).astype(o_ref.dtype)


def _modulate(x, gain, mod3, shift_k, scale_k, seq, tm=512):
    m, d = x.shape
    per_b = seq // tm
    return pl.pallas_call(
        _modulate_kernel,
        grid=(m // tm,),
        in_specs=[
            pl.BlockSpec((tm, d), lambda i: (i, 0)),
            pl.BlockSpec((1, d), lambda i: (0, 0)),
            pl.BlockSpec((1, 1, d), lambda i: ((i // per_b) * 6 + shift_k, 0, 0)),
            pl.BlockSpec((1, 1, d), lambda i: ((i // per_b) * 6 + scale_k, 0, 0)),
        ],
        out_specs=pl.BlockSpec((tm, d), lambda i: (i, 0)),
        out_shape=jax.ShapeDtypeStruct((m, d), BF16),
        compiler_params=_params("parallel"),
        name="modulate",
    )(x, gain.reshape(1, d), mod3, mod3)


def _rmsnorm_kernel(x_ref, g_ref, o_ref):
    x = x_ref[...].astype(F32)
    ms = jnp.mean(x * x, axis=-1, keepdims=True)
    o_ref[...] = (x * lax.rsqrt(ms + EPS) * g_ref[...]).astype(o_ref.dtype)


def _rmsnorm(x, gain, out_dtype, col_blk=0, tm=512):
    m = x.shape[0]
    d = gain.shape[0]
    return pl.pallas_call(
        _rmsnorm_kernel,
        grid=(m // tm,),
        in_specs=[
            pl.BlockSpec((tm, d), lambda i: (i, col_blk)),
            pl.BlockSpec((1, d), lambda i: (0, 0)),
        ],
        out_specs=pl.BlockSpec((tm, d), lambda i: (i, 0)),
        out_shape=jax.ShapeDtypeStruct((m, d), out_dtype),
        compiler_params=_params("parallel"),
        name="rmsnorm",
    )(x, gain.reshape(1, d))


def _mm_kernel(a_ref, w_ref, o_ref, acc_ref):
    k = pl.program_id(2)

    @pl.when(k == 0)
    def _():
        acc_ref[...] = jnp.zeros_like(acc_ref)

    acc_ref[...] += jnp.dot(a_ref[...], w_ref[...], preferred_element_type=F32)

    @pl.when(k == pl.num_programs(2) - 1)
    def _():
        o_ref[...] = acc_ref[...].astype(o_ref.dtype)


def _matmul(a, w, out_dtype, tm=1024, tn=512, tk=2048):
    m, kd = a.shape
    n = w.shape[1]
    tk = min(tk, kd)
    return pl.pallas_call(
        _mm_kernel,
        grid=(m // tm, n // tn, kd // tk),
        in_specs=[
            pl.BlockSpec((tm, tk), lambda i, j, k: (i, k)),
            pl.BlockSpec((tk, tn), lambda i, j, k: (k, j)),
        ],
        out_specs=pl.BlockSpec((tm, tn), lambda i, j, k: (i, j)),
        out_shape=jax.ShapeDtypeStruct((m, n), out_dtype),
        scratch_shapes=[pltpu.VMEM((tm, tn), F32)],
        compiler_params=_params("parallel", "parallel", "arbitrary"),
        name="matmul",
    )(a, w)


def _mm_swiglu_kernel(a_ref, wg_ref, wu_ref, o_ref, accg_ref, accu_ref):
    k = pl.program_id(2)

    @pl.when(k == 0)
    def _():
        accg_ref[...] = jnp.zeros_like(accg_ref)
        accu_ref[...] = jnp.zeros_like(accu_ref)

    a = a_ref[...]
    accg_ref[...] += jnp.dot(a, wg_ref[...], preferred_element_type=F32)
    accu_ref[...] += jnp.dot(a, wu_ref[...], preferred_element_type=F32)

    @pl.when(k == pl.num_programs(2) - 1)
    def _():
        g = accg_ref[...]
        o_ref[...] = (g * jax.nn.sigmoid(g) * accu_ref[...]).astype(o_ref.dtype)


def _matmul_swiglu(a, w_gate_up, tm=1024, tn=512, tk=2048):
    m, kd = a.shape
    f = w_gate_up.shape[1] // 2
    tk = min(tk, kd)
    up_off = f // tn
    return pl.pallas_call(
        _mm_swiglu_kernel,
        grid=(m // tm, f // tn, kd // tk),
        in_specs=[
            pl.BlockSpec((tm, tk), lambda i, j, k: (i, k)),
            pl.BlockSpec((tk, tn), lambda i, j, k: (k, j)),
            pl.BlockSpec((tk, tn), lambda i, j, k: (k, j + up_off)),
        ],
        out_specs=pl.BlockSpec((tm, tn), lambda i, j, k: (i, j)),
        out_shape=jax.ShapeDtypeStruct((m, f), BF16),
        scratch_shapes=[pltpu.VMEM((tm, tn), F32), pltpu.VMEM((tm, tn), F32)],
        compiler_params=_params("parallel", "parallel", "arbitrary"),
        name="matmul_swiglu",
    )(a, w_gate_up, w_gate_up)


def _mm_resid_kernel(a_ref, w_ref, x_ref, gate_ref, o_ref, acc_ref):
    k = pl.program_id(2)

    @pl.when(k == 0)
    def _():
        acc_ref[...] = jnp.zeros_like(acc_ref)

    acc_ref[...] += jnp.dot(a_ref[...], w_ref[...], preferred_element_type=F32)

    @pl.when(k == pl.num_programs(2) - 1)
    def _():
        o_ref[...] = x_ref[...] + gate_ref[0] * acc_ref[...]


def _matmul_resid(a, w, x, mod3, gate_k, seq, tm=1024, tn=512, tk=512):
    m, kd = a.shape
    n = w.shape[1]
    tk = min(tk, kd)
    per_b = seq // tm
    return pl.pallas_call(
        _mm_resid_kernel,
        grid=(m // tm, n // tn, kd // tk),
        in_specs=[
            pl.BlockSpec((tm, tk), lambda i, j, k: (i, k)),
            pl.BlockSpec((tk, tn), lambda i, j, k: (k, j)),
            pl.BlockSpec((tm, tn), lambda i, j, k: (i, j)),
            pl.BlockSpec((1, 1, tn), lambda i, j, k: ((i // per_b) * 6 + gate_k, 0, j)),
        ],
        out_specs=pl.BlockSpec((tm, tn), lambda i, j, k: (i, j)),
        out_shape=jax.ShapeDtypeStruct((m, n), F32),
        scratch_shapes=[pltpu.VMEM((tm, tn), F32)],
        compiler_params=_params("parallel", "parallel", "arbitrary"),
        name="matmul_resid",
    )(a, w, x, mod3)


def _headmm_kernel(a_ref, w_ref, o_ref, *, scale):
    acc = jnp.dot(a_ref[...], w_ref[0], preferred_element_type=F32)
    o_ref[...] = (acc * scale).astype(o_ref.dtype)


def _head_matmul(a, w, col0, scale=1.0, tm=1024):
    m = a.shape[0]
    h, din, dout = w.shape
    return pl.pallas_call(
        functools.partial(_headmm_kernel, scale=scale),
        grid=(m // tm, h),
        in_specs=[
            pl.BlockSpec((tm, din), lambda i, j: (i, col0 + j)),
            pl.BlockSpec((1, din, dout), lambda i, j: (j, 0, 0)),
        ],
        out_specs=pl.BlockSpec((tm, dout), lambda i, j: (i, j)),
        out_shape=jax.ShapeDtypeStruct((m, h * dout), BF16),
        compiler_params=_params("parallel", "parallel"),
        name="head_matmul",
    )(a, w)


def _nt_dot(a, b):
    return lax.dot_general(a, b, (((1,), (1,)), ((), ())), preferred_element_type=F32)


def _dilated_kernel(slope_ref, q_ref, k_ref, v_ref, o_ref, acc_ref, lse_ref, *, seq, dilations):
    slope = slope_ref[0][:, :1]
    scale = HEAD_DIM ** -0.5
    qi = lax.broadcasted_iota(jnp.int32, (BLOCK, BLOCK), 0)
    ki = lax.broadcasted_iota(jnp.int32, (BLOCK, BLOCK), 1)
    rel_cur = (qi - ki).astype(F32)
    rel_prev = rel_cur + float(BLOCK)
    ok_cur = qi >= ki
    ok_prev = ki >= qi

    def tile(g, dil, start, prev_start, has_prev):
        rows = pl.ds(start, BLOCK, stride=dil)
        q = (q_ref[0, rows, :] * scale).astype(BF16)
        kc = k_ref[0, rows, :].astype(BF16)
        vc = v_ref[0, rows, :].astype(BF16)
        neg = -slope * float(dil)
        s_c = _nt_dot(q, kc) + jnp.where(ok_cur, rel_cur * neg, NEG_BIG)
        m = jnp.max(s_c, axis=-1, keepdims=True)
        if has_prev:
            prows = pl.ds(prev_start, BLOCK, stride=dil)
            kp = k_ref[0, prows, :].astype(BF16)
            vp = v_ref[0, prows, :].astype(BF16)
            s_p = _nt_dot(q, kp) + jnp.where(ok_prev, rel_prev * neg, NEG_BIG)
            m = jnp.maximum(m, jnp.max(s_p, axis=-1, keepdims=True))
        p_c = jnp.exp(s_c - m)
        l = jnp.sum(p_c, axis=-1, keepdims=True)
        acc = jnp.dot(p_c.astype(BF16), vc, preferred_element_type=F32)
        if has_prev:
            p_p = jnp.exp(s_p - m)
            l = l + jnp.sum(p_p, axis=-1, keepdims=True)
            acc = acc + jnp.dot(p_p.astype(BF16), vp, preferred_element_type=F32)
        acc_ref[g, rows, :] = acc / l
        lse_ref[g, rows, :] = jnp.broadcast_to(m + jnp.log(l), (BLOCK, LANE))

    for g, dil in enumerate(dilations):
        n_tiles = seq // dil // BLOCK
        step = BLOCK * dil

        def first(r, carry, g=g, dil=dil):
            tile(g, dil, r, 0, False)
            return carry

        lax.fori_loop(0, dil, first, 0)

        if n_tiles > 1:
            def rest(i, carry, g=g, dil=dil, step=step):
                n = 1 + i // dil
                r = i % dil
                tile(g, dil, n * step + r, (n - 1) * step + r, True)
                return carry

            lax.fori_loop(0, (n_tiles - 1) * dil, rest, 0)

    n_g = len(dilations)
    lse_max = lse_ref[0]
    for g in range(1, n_g):
        lse_max = jnp.maximum(lse_max, lse_ref[g])
    den = jnp.zeros_like(lse_max)
    num = jnp.zeros_like(lse_max)
    for g in range(n_g):
        w = jnp.exp(lse_ref[g] - lse_max)
        den = den + w
        num = num + w * acc_ref[g]
    o_ref[0] = (num / den).astype(o_ref.dtype)


def _dilated_attention(qkv, slopes3, bsz, seq):
    dilations = tuple(d for _, d in DILATED_BRANCHES)
    for w, d in DILATED_BRANCHES:
        assert w // d == BLOCK and seq % (d * BLOCK) == 0
    g = len(dilations)
    h = N_HEADS
    kern = functools.partial(_dilated_kernel, seq=seq, dilations=dilations)
    return pl.pallas_call(
        kern,
        grid=(bsz, h),
        in_specs=[
            pl.BlockSpec((1, 1, LANE), lambda b, j: (j, 0, 0)),
            pl.BlockSpec((1, seq, HEAD_DIM), lambda b, j: (b, 0, j)),
            pl.BlockSpec((1, seq, HEAD_DIM), lambda b, j: (b, 0, h + j)),
            pl.BlockSpec((1, seq, HEAD_DIM), lambda b, j: (b, 0, 2 * h + j)),
        ],
        out_specs=pl.BlockSpec((1, seq, HEAD_DIM), lambda b, j: (b, 0, j)),
        out_shape=jax.ShapeDtypeStruct((bsz, seq, h * HEAD_DIM), BF16),
        scratch_shapes=[
            pltpu.VMEM((g, seq, HEAD_DIM), F32),
            pltpu.VMEM((g, seq, LANE), F32),
        ],
        compiler_params=_params("parallel", "parallel"),
        name="dilated_attention",
    )(slopes3, qkv, qkv, qkv)


def _count_ge(keys, cand):
    return jnp.sum(jnp.where(keys >= cand, 1.0, 0.0), axis=-1, keepdims=True)


def _dsa_kernel(slope_ref, qidx_ref, kidx_ref, widx_ref, qlat_ref, ckv_ref, o_ref, *, seq, topk, tq):
    t = pl.program_id(1)
    q_pos = t * tq + lax.broadcasted_iota(jnp.int32, (tq, seq), 0)
    k_pos = lax.broadcasted_iota(jnp.int32, (tq, seq), 1)
    causal = k_pos <= q_pos

    kidx = kidx_ref[0]
    w_all = widx_ref[0].astype(F32) * (IDX_DIM ** -0.5 * IDX_HEADS ** -0.5)
    score = jnp.zeros((tq, seq), F32)
    for h in range(IDX_HEADS):
        logits = _nt_dot(qidx_ref[0, :, h * IDX_DIM:(h + 1) * IDX_DIM], kidx)
        score = score + w_all[:, h:h + 1] * jnp.maximum(logits, 0.0)
    score = jnp.where(causal, score, NEG_BIG)
    score = jnp.where(score == 0.0, 0.0, score)

    bits = lax.bitcast_convert_type(score, jnp.int32)
    keys = jnp.where(bits < 0, bits ^ jnp.int32(0x7FFFFFFF), bits)

    kf = float(topk)
    int_min = jnp.int32(-(2 ** 31))
    thr = jnp.where(_count_ge(keys, jnp.int32(0)) >= kf, jnp.int32(0), int_min)

    def bit_step(i, thr):
        cand = thr | (jnp.int32(1) << (30 - i))
        return jnp.where(_count_ge(keys, cand) >= kf, cand, thr)

    thr = lax.fori_loop(0, 31, bit_step, thr)

    gt = keys > thr
    eq = keys == thr
    n_gt = jnp.sum(jnp.where(gt, 1.0, 0.0), axis=-1, keepdims=True)
    n_eq = jnp.sum(jnp.where(eq, 1.0, 0.0), axis=-1, keepdims=True)
    need = kf - n_gt
    tie_rows = jnp.max(jnp.where(n_eq > need, 1.0, 0.0))

    def tie_cut():
        def cut_step(i, cut):
            cand = cut - (jnp.int32(1) << (n_bits - 1 - i))
            cnt = jnp.sum(jnp.where(eq & (k_pos <= cand), 1.0, 0.0), axis=-1, keepdims=True)
            return jnp.where(cnt >= need, cand, cut)

        n_bits = (seq - 1).bit_length()
        return lax.fori_loop(0, n_bits, cut_step, jnp.full((tq, 1), (1 << n_bits) - 1, jnp.int32))

    cut = lax.cond(tie_rows > 0.0, tie_cut, lambda: jnp.full((tq, 1), seq, jnp.int32))
    selected = (gt | (eq & (k_pos <= cut))) & causal
    dist = jnp.where(selected, (q_pos - k_pos).astype(F32), MASK_DIST)

    ckv = ckv_ref[0]
    for h in range(N_HEADS):
        slope = slope_ref[h][:, :1]
        ql = qlat_ref[0, :, h * KV_LATENT:(h + 1) * KV_LATENT]
        s = _nt_dot(ql, ckv) - slope * dist
        m = jnp.max(s, axis=-1, keepdims=True)
        p = jnp.exp(s - m)
        l = jnp.sum(p, axis=-1, keepdims=True)
        o = jnp.dot(p.astype(BF16), ckv, preferred_element_type=F32) / l
        o_ref[0, :, h * KV_LATENT:(h + 1) * KV_LATENT] = o.astype(o_ref.dtype)


def _dsa_attention(proj, qlat, ckv, slopes3, bsz, seq, topk, tq=128):
    h = N_HEADS
    qidx_blk = _DSA_COLS["q_idx"] // (IDX_HEADS * IDX_DIM)
    kidx_blk = _DSA_COLS["k_idx"] // IDX_DIM
    widx_blk = _DSA_COLS["w_idx"] // LANE
    kern = functools.partial(_dsa_kernel, seq=seq, topk=topk, tq=tq)
    return pl.pallas_call(
        kern,
        grid=(bsz, seq // tq),
        in_specs=[
            pl.BlockSpec((h, 1, LANE), lambda b, t: (0, 0, 0)),
            pl.BlockSpec((1, tq, IDX_HEADS * IDX_DIM), lambda b, t: (b, t, qidx_blk)),
            pl.BlockSpec((1, seq, IDX_DIM), lambda b, t: (b, 0, kidx_blk)),
            pl.BlockSpec((1, tq, LANE), lambda b, t: (b, t, widx_blk)),
            pl.BlockSpec((1, tq, h * KV_LATENT), lambda b, t: (b, t, 0)),
            pl.BlockSpec((1, seq, KV_LATENT), lambda b, t: (b, 0, 0)),
        ],
        out_specs=pl.BlockSpec((1, tq, h * KV_LATENT), lambda b, t: (b, t, 0)),
        out_shape=jax.ShapeDtypeStruct((bsz, seq, h * KV_LATENT), BF16),
        compiler_params=_params("parallel", "parallel"),
        name="dsa_attention",
    )(slopes3, proj, proj, proj, qlat, ckv)


def _dsa_in_weight(w_in):
    sizes = (N_HEADS * HEAD_DIM, KV_LATENT, IDX_HEADS * IDX_DIM, IDX_DIM, IDX_HEADS)
    q, c_kv, q_idx, k_idx, w_idx = jnp.split(w_in, [sum(sizes[:i + 1]) for i in range(4)], axis=1)
    pad = jnp.zeros((w_in.shape[0], _DSA_WIDTH - sum(sizes)), w_in.dtype)
    return jnp.concatenate([q, q_idx, c_kv, k_idx, w_idx, pad], axis=1)


def _dilated_layer(h, w_in, slopes3, bsz, seq):
    qkv = _matmul(h, w_in.astype(BF16), F32)
    return _dilated_attention(qkv.reshape(bsz, seq, -1), slopes3, bsz, seq).reshape(bsz * seq, -1)


def _dsa_layer(h, w_in, kv_norm, w_uk, w_uv, slopes3, bsz, seq, topk):
    m = bsz * seq
    proj = _matmul(h, _dsa_in_weight(w_in).astype(BF16), BF16)
    ckv = _rmsnorm(proj, kv_norm, BF16, col_blk=_DSA_COLS["c_kv"] // KV_LATENT)
    qlat = _head_matmul(proj, w_uk.astype(BF16), 0, scale=HEAD_DIM ** -0.5)
    o_lat = _dsa_attention(
        proj.reshape(bsz, seq, -1), qlat.reshape(bsz, seq, -1), ckv.reshape(bsz, seq, -1), slopes3, bsz, seq, topk
    )
    return _head_matmul(o_lat.reshape(m, -1), w_uv.astype(BF16), 0)


def _forward(x, c, layers, final_norm):
    bsz, seq, d = x.shape
    m = bsz * seq
    topk = min(TOPK_MAX, seq // 4)
    slopes = jnp.exp2(-8.0 * jnp.arange(1, N_HEADS + 1, dtype=F32) / N_HEADS)
    slopes3 = jnp.broadcast_to(slopes[:, None, None], (N_HEADS, 1, LANE))
    xf = x.reshape(m, d)
    for i, (norm_attn, ada_w, ada_b, mixer_w, norm_ffn, w_gate_up, w_down) in enumerate(layers):
        mod3 = _adaln(c, ada_w, ada_b).reshape(bsz * 6, 1, d)
        h = _modulate(xf, norm_attn, mod3, 0, 1, seq)
        if i % 2 == 0:
            w_in, w_out = mixer_w
            o = _dilated_layer(h, w_in, slopes3, bsz, seq)
        else:
            w_in, kv_norm, w_uk, w_uv, w_out = mixer_w
            o = _dsa_layer(h, w_in, kv_norm, w_uk, w_uv, slopes3, bsz, seq, topk)
        xf = _matmul_resid(o, w_out.astype(BF16), xf, mod3, 2, seq)
        h = _modulate(xf, norm_ffn, mod3, 3, 4, seq)
        act = _matmul_swiglu(h, w_gate_up.astype(BF16))
        xf = _matmul_resid(act, w_down.astype(BF16), xf, mod3, 5, seq)
    return _rmsnorm(xf, final_norm, x.dtype).reshape(bsz, seq, d)


def kernel(x, c, l0_norm_attn, l0_ada_w, l0_ada_b, l0_w_in, l0_w_out, l0_norm_ffn, l0_w_gate_up, l0_w_down, l1_norm_attn, l1_ada_w, l1_ada_b, l1_w_in, l1_kv_norm, l1_w_uk, l1_w_uv, l1_w_out, l1_norm_ffn, l1_w_gate_up, l1_w_down, l2_norm_attn, l2_ada_w, l2_ada_b, l2_w_in, l2_w_out, l2_norm_ffn, l2_w_gate_up, l2_w_down, l3_norm_attn, l3_ada_w, l3_ada_b, l3_w_in, l3_kv_norm, l3_w_uk, l3_w_uv, l3_w_out, l3_norm_ffn, l3_w_gate_up, l3_w_down, final_norm):
    layers = (
        (l0_norm_attn, l0_ada_w, l0_ada_b, (l0_w_in, l0_w_out), l0_norm_ffn, l0_w_gate_up, l0_w_down),
        (l1_norm_attn, l1_ada_w, l1_ada_b, (l1_w_in, l1_kv_norm, l1_w_uk, l1_w_uv, l1_w_out),
         l1_norm_ffn, l1_w_gate_up, l1_w_down),
        (l2_norm_attn, l2_ada_w, l2_ada_b, (l2_w_in, l2_w_out), l2_norm_ffn, l2_w_gate_up, l2_w_down),
        (l3_norm_attn, l3_ada_w, l3_ada_b, (l3_w_in, l3_kv_norm, l3_w_uk, l3_w_uv, l3_w_out),
         l3_norm_ffn, l3_w_gate_up, l3_w_down),
    )
    return _forward(x, c, layers, final_norm)
```

```python
import functools

import jax
import jax.numpy as jnp
from jax import lax
from jax.experimental import pallas as pl
from jax.experimental.pallas import tpu as pltpu

N_HEADS = 16
HEAD_DIM = 128
DILATED_BRANCHES = ((128, 1), (512, 4), (2048, 16))
BLOCK = 128
KV_LATENT = 256
IDX_HEADS = 16
IDX_DIM = 128
TOPK_MAX = 256
EPS = 1e-6
NEG_BIG = -1e30
LOG2E = 1.4426950408889634

LANE = 128
VMEM_LIMIT = 56 * 1024 * 1024

F32 = jnp.float32
BF16 = jnp.bfloat16

_DSA_COLS = {
    "q": 0,
    "q_idx": N_HEADS * HEAD_DIM,
    "c_kv": N_HEADS * HEAD_DIM + IDX_HEADS * IDX_DIM,
    "k_idx": N_HEADS * HEAD_DIM + IDX_HEADS * IDX_DIM + KV_LATENT,
    "w_idx": N_HEADS * HEAD_DIM + IDX_HEADS * IDX_DIM + KV_LATENT + IDX_DIM,
}
_DSA_WIDTH = -(-(_DSA_COLS["w_idx"] + IDX_HEADS) // 512) * 512


def _params(*sem):
    return pltpu.CompilerParams(dimension_semantics=sem, vmem_limit_bytes=VMEM_LIMIT)


def _adaln_kernel(c_ref, w_ref, b_ref, o_ref):
    c = c_ref[...]
    sc = (c * jax.nn.sigmoid(c)).astype(BF16)
    acc = jnp.dot(sc, w_ref[...].astype(BF16), preferred_element_type=F32)
    o_ref[...] = acc + b_ref[...]


def _adaln(c, w, b, tn=1024):
    bsz, d = c.shape
    n = w.shape[1]
    return pl.pallas_call(
        _adaln_kernel,
        grid=(n // tn,),
        in_specs=[
            pl.BlockSpec((bsz, d), lambda j: (0, 0)),
            pl.BlockSpec((d, tn), lambda j: (0, j)),
            pl.BlockSpec((1, tn), lambda j: (0, j)),
        ],
        out_specs=pl.BlockSpec((bsz, tn), lambda j: (0, j)),
        out_shape=jax.ShapeDtypeStruct((bsz, n), F32),
        compiler_params=_params("parallel"),
        name="adaln",
    )(c, w, b.reshape(1, n))


def _modulate_kernel(x_ref, g_ref, shift_ref, scale_ref, o_ref):
    x = x_ref[...]
    ms = jnp.mean(x * x, axis=-1, keepdims=True)
    y = x * lax.rsqrt(ms + EPS) * g_ref[...]
    o_ref[...] = (y * (1.0 + scale_ref[0]) + shift_ref[0]).astype(o_ref.dtype)


def _modulate(x, gain, mod3, shift_k, scale_k, seq, tm=512):
    m, d = x.shape
    per_b = seq // tm
    return pl.pallas_call(
        _modulate_kernel,
        grid=(m // tm,),
        in_specs=[
            pl.BlockSpec((tm, d), lambda i: (i, 0)),
            pl.BlockSpec((1, d), lambda i: (0, 0)),
            pl.BlockSpec((1, 1, d), lambda i: ((i // per_b) * 6 + shift_k, 0, 0)),
            pl.BlockSpec((1, 1, d), lambda i: ((i // per_b) * 6 + scale_k, 0, 0)),
        ],
        out_specs=pl.BlockSpec((tm, d), lambda i: (i, 0)),
        out_shape=jax.ShapeDtypeStruct((m, d), BF16),
        compiler_params=_params("parallel"),
        name="modulate",
    )(x, gain.reshape(1, d), mod3, mod3)


def _rmsnorm_kernel(x_ref, g_ref, o_ref):
    x = x_ref[...].astype(F32)
    ms = jnp.mean(x * x, axis=-1, keepdims=True)
    o_ref[...] = (x * lax.rsqrt(ms + EPS) * g_ref[...]).astype(o_ref.dtype)


def _rmsnorm(x, gain, out_dtype, col_blk=0, tm=512):
    m = x.shape[0]
    d = gain.shape[0]
    return pl.pallas_call(
        _rmsnorm_kernel,
        grid=(m // tm,),
        in_specs=[
            pl.BlockSpec((tm, d), lambda i: (i, col_blk)),
            pl.BlockSpec((1, d), lambda i: (0, 0)),
        ],
        out_specs=pl.BlockSpec((tm, d), lambda i: (i, 0)),
        out_shape=jax.ShapeDtypeStruct((m, d), out_dtype),
        compiler_params=_params("parallel"),
        name="rmsnorm",
    )(x, gain.reshape(1, d))


def _accumulate(acc_refs, products, finish):
    if not acc_refs:
        finish(products())
        return
    k = pl.program_id(2)

    @pl.when(k == 0)
    def _():
        for r in acc_refs:
            r[...] = jnp.zeros_like(r)

    for r, p in zip(acc_refs, products()):
        r[...] += p

    @pl.when(k == pl.num_programs(2) - 1)
    def _():
        finish([r[...] for r in acc_refs])


def _mm_kernel(a_ref, w_ref, o_ref, *acc):
    def products():
        return [jnp.dot(a_ref[...], w_ref[...], preferred_element_type=F32)]

    def finish(v):
        o_ref[...] = v[0].astype(o_ref.dtype)

    _accumulate(acc, products, finish)


def _mm_swiglu_kernel(a_ref, wg_ref, wu_ref, o_ref, *acc):
    def products():
        a = a_ref[...]
        return [jnp.dot(a, wg_ref[...], preferred_element_type=F32),
                jnp.dot(a, wu_ref[...], preferred_element_type=F32)]

    def finish(v):
        g, u = v
        o_ref[...] = (g * jax.nn.sigmoid(g) * u).astype(o_ref.dtype)

    _accumulate(acc, products, finish)


def _mm_resid_kernel(a_ref, w_ref, x_ref, gate_ref, o_ref, *acc):
    def products():
        return [jnp.dot(a_ref[...], w_ref[...], preferred_element_type=F32)]

    def finish(v):
        o_ref[...] = x_ref[...] + gate_ref[0] * v[0]

    _accumulate(acc, products, finish)


def _mm_call(kern, name, a, ws, extra_specs, extra_args, n, out_dtype, tm, tn, tk, w_col_offs):
    m, kd = a.shape
    tk = min(tk, kd)
    nk = kd // tk
    assert m % tm == 0 and n % tn == 0 and kd % tk == 0
    w_specs = [pl.BlockSpec((tk, tn), lambda i, j, k, off=off: (k, j + off)) for off in w_col_offs]
    return pl.pallas_call(
        kern,
        grid=(m // tm, n // tn, nk),
        in_specs=[pl.BlockSpec((tm, tk), lambda i, j, k: (i, k))] + w_specs + extra_specs,
        out_specs=pl.BlockSpec((tm, tn), lambda i, j, k: (i, j)),
        out_shape=jax.ShapeDtypeStruct((m, n), out_dtype),
        scratch_shapes=[pltpu.VMEM((tm, tn), F32) for _ in ws] if nk > 1 else [],
        compiler_params=_params("parallel", "parallel", "arbitrary"),
        name=name,
    )(a, *ws, *extra_args)


def _matmul(a, w, out_dtype, tm=1024, tn=512, tk=2048):
    return _mm_call(_mm_kernel, "matmul", a, [w], [], [], w.shape[1], out_dtype, tm, tn, tk, [0])


def _matmul_swiglu(a, w_gate_up, tm=1024, tn=512, tk=2048):
    f = w_gate_up.shape[1] // 2
    return _mm_call(_mm_swiglu_kernel, "matmul_swiglu", a, [w_gate_up, w_gate_up], [], [], f, BF16,
                    tm, tn, tk, [0, f // tn])


def _matmul_resid(a, w, x, mod3, gate_k, seq, tm=1024, tn=512, tk=2048):
    per_b = seq // tm
    extra_specs = [
        pl.BlockSpec((tm, tn), lambda i, j, k: (i, j)),
        pl.BlockSpec((1, 1, tn), lambda i, j, k: ((i // per_b) * 6 + gate_k, 0, j)),
    ]
    return _mm_call(_mm_resid_kernel, "matmul_resid", a, [w], extra_specs, [x, mod3], w.shape[1], F32,
                    tm, tn, tk, [0])


def _headmm_kernel(a_ref, w_ref, o_ref, *, scale):
    acc = jnp.dot(a_ref[...], w_ref[0], preferred_element_type=F32)
    o_ref[...] = (acc * scale).astype(o_ref.dtype)


def _head_matmul(a, w, col0, scale=1.0, tm=1024):
    m = a.shape[0]
    h, din, dout = w.shape
    return pl.pallas_call(
        functools.partial(_headmm_kernel, scale=scale),
        grid=(m // tm, h),
        in_specs=[
            pl.BlockSpec((tm, din), lambda i, j: (i, col0 + j)),
            pl.BlockSpec((1, din, dout), lambda i, j: (j, 0, 0)),
        ],
        out_specs=pl.BlockSpec((tm, dout), lambda i, j: (i, j)),
        out_shape=jax.ShapeDtypeStruct((m, h * dout), BF16),
        compiler_params=_params("parallel", "parallel"),
        name="head_matmul",
    )(a, w)


def _nt_dot(a, b):
    return lax.dot_general(a, b, (((1,), (1,)), ((), ())), preferred_element_type=F32)


def _dilated_kernel(slope_ref, q_ref, k_ref, v_ref, o_ref, bias_ref, vaug_ref, *, seq, branches):
    n_tiles = seq // BLOCK

    @pl.when(pl.program_id(1) == 0)
    def _():
        slope = slope_ref[0][:, :1]
        qi = lax.broadcasted_iota(jnp.int32, (BLOCK, seq), 0)
        kj = lax.broadcasted_iota(jnp.int32, (BLOCK, seq), 1)
        d = qi + (n_tiles - 1) * BLOCK - kj
        count = jnp.zeros((BLOCK, seq), F32)
        for window, dil in branches:
            assert dil & (dil - 1) == 0
            count = count + jnp.where(((d & (dil - 1)) == 0) & (d <= window), 1.0, 0.0)
        bias = jnp.log2(jnp.maximum(count, 1.0)) - (slope * LOG2E) * d.astype(F32)
        bias_ref[...] = jnp.where((d >= 0) & (count > 0.0), bias, NEG_BIG)
        vaug_ref[:, HEAD_DIM:] = jnp.ones((seq, LANE), BF16)

    vaug_ref[:, :HEAD_DIM] = v_ref[0]

    for n in range(n_tiles):
        n_keys = (n + 1) * BLOCK
        q = (q_ref[0, n * BLOCK:(n + 1) * BLOCK, :].astype(F32) * (HEAD_DIM ** -0.5 * LOG2E)).astype(BF16)
        s = _nt_dot(q, k_ref[0, :n_keys, :]) + bias_ref[:, seq - n_keys:]
        m = jnp.max(s, axis=-1, keepdims=True)
        p = jnp.exp2(s - m).astype(BF16)
        pv = jnp.dot(p, vaug_ref[:n_keys, :], preferred_element_type=F32)
        o_ref[0, n * BLOCK:(n + 1) * BLOCK, :] = (pv[:, :HEAD_DIM] / pv[:, HEAD_DIM:]).astype(o_ref.dtype)


def _dilated_attention(qkv, slopes3, bsz, seq):
    h = N_HEADS
    kern = functools.partial(_dilated_kernel, seq=seq, branches=DILATED_BRANCHES)
    return pl.pallas_call(
        kern,
        grid=(h, bsz),
        in_specs=[
            pl.BlockSpec((1, 1, LANE), lambda j, b: (j, 0, 0)),
            pl.BlockSpec((1, seq, HEAD_DIM), lambda j, b: (b, 0, j)),
            pl.BlockSpec((1, seq, HEAD_DIM), lambda j, b: (b, 0, h + j)),
            pl.BlockSpec((1, seq, HEAD_DIM), lambda j, b: (b, 0, 2 * h + j)),
        ],
        out_specs=pl.BlockSpec((1, seq, HEAD_DIM), lambda j, b: (b, 0, j)),
        out_shape=jax.ShapeDtypeStruct((bsz, seq, h * HEAD_DIM), BF16),
        scratch_shapes=[
            pltpu.VMEM((BLOCK, seq), F32),
            pltpu.VMEM((seq, HEAD_DIM + LANE), BF16),
        ],
        compiler_params=_params("parallel", "arbitrary"),
        name="dilated_attention",
    )(slopes3, qkv, qkv, qkv)


def _count_ge(keys, cand):
    return jnp.sum(jnp.where(keys >= cand, 1.0, 0.0), axis=-1, keepdims=True)


def _dsa_kernel(slope_ref, qidx_ref, kidx_ref, widx_ref, qlat_ref, ckv_ref, o_ref, *, seq, topk, tq, n_groups):
    tiles_per_group = seq // tq // n_groups
    group = pl.program_id(1) // tiles_per_group
    for g in range(n_groups):
        body = functools.partial(_dsa_body, slope_ref, qidx_ref, kidx_ref, widx_ref, qlat_ref, ckv_ref, o_ref,
                                 seq=(g + 1) * tiles_per_group * tq, topk=topk, tq=tq)
        pl.when(group == g)(body)


def _dsa_body(slope_ref, qidx_ref, kidx_ref, widx_ref, qlat_ref, ckv_ref, o_ref, *, seq, topk, tq):
    t = pl.program_id(1)
    q_pos = t * tq + lax.broadcasted_iota(jnp.int32, (tq, seq), 0)
    k_pos = lax.broadcasted_iota(jnp.int32, (tq, seq), 1)
    causal = k_pos <= q_pos

    kidx = kidx_ref[0, :seq, :]
    w_all = widx_ref[0].astype(F32) * (IDX_DIM ** -0.5 * IDX_HEADS ** -0.5)
    score = jnp.zeros((tq, seq), F32)
    for h in range(IDX_HEADS):
        logits = _nt_dot(qidx_ref[0, :, h * IDX_DIM:(h + 1) * IDX_DIM], kidx)
        score = score + w_all[:, h:h + 1] * jnp.maximum(logits, 0.0)
    score = jnp.where(causal, score, NEG_BIG)
    score = jnp.where(score == 0.0, 0.0, score)

    bits = lax.bitcast_convert_type(score, jnp.int32)
    keys = jnp.where(bits < 0, bits ^ jnp.int32(0x7FFFFFFF), bits)

    kf = float(topk)
    int_min = jnp.int32(-(2 ** 31))
    thr = jnp.where(_count_ge(keys, jnp.int32(0)) >= kf, jnp.int32(0), int_min)

    def bit_step(i, thr):
        cand = thr | (jnp.int32(1) << (30 - i))
        return jnp.where(_count_ge(keys, cand) >= kf, cand, thr)

    thr = lax.fori_loop(0, 31, bit_step, thr)

    gt = keys > thr
    eq = keys == thr
    n_gt = jnp.sum(jnp.where(gt, 1.0, 0.0), axis=-1, keepdims=True)
    n_eq = jnp.sum(jnp.where(eq, 1.0, 0.0), axis=-1, keepdims=True)
    need = kf - n_gt
    tie_rows = jnp.max(jnp.where(n_eq > need, 1.0, 0.0))

    def tie_cut():
        def cut_step(i, cut):
            cand = cut - (jnp.int32(1) << (n_bits - 1 - i))
            cnt = jnp.sum(jnp.where(eq & (k_pos <= cand), 1.0, 0.0), axis=-1, keepdims=True)
            return jnp.where(cnt >= need, cand, cut)

        n_bits = (seq - 1).bit_length()
        return lax.fori_loop(0, n_bits, cut_step, jnp.full((tq, 1), (1 << n_bits) - 1, jnp.int32))

    cut = lax.cond(tie_rows > 0.0, tie_cut, lambda: jnp.full((tq, 1), seq, jnp.int32))
    selected = (gt | (eq & (k_pos <= cut))) & causal
    mask_bias = jnp.where(selected, 0.0, NEG_BIG)

    k_pos_row = lax.broadcasted_iota(jnp.int32, (1, seq), 1).astype(F32)
    ckv = ckv_ref[0, :seq, :]
    for h in range(N_HEADS):
        slope = slope_ref[h][:, :1]
        ql = qlat_ref[0, :, h * KV_LATENT:(h + 1) * KV_LATENT]
        s = _nt_dot(ql, ckv) + (slope * LOG2E) * k_pos_row + mask_bias
        m = jnp.max(s, axis=-1, keepdims=True)
        p = jnp.exp2(s - m)
        l = jnp.sum(p, axis=-1, keepdims=True)
        o = jnp.dot(p.astype(BF16), ckv, preferred_element_type=F32) / l
        o_ref[0, :, h * KV_LATENT:(h + 1) * KV_LATENT] = o.astype(o_ref.dtype)


def _dsa_attention(proj, qlat, ckv, slopes3, bsz, seq, topk, tq=128, n_groups=4):
    h = N_HEADS
    qidx_blk = _DSA_COLS["q_idx"] // (IDX_HEADS * IDX_DIM)
    kidx_blk = _DSA_COLS["k_idx"] // IDX_DIM
    widx_blk = _DSA_COLS["w_idx"] // LANE
    kern = functools.partial(_dsa_kernel, seq=seq, topk=topk, tq=tq, n_groups=n_groups)
    return pl.pallas_call(
        kern,
        grid=(bsz, seq // tq),
        in_specs=[
            pl.BlockSpec((h, 1, LANE), lambda b, t: (0, 0, 0)),
            pl.BlockSpec((1, tq, IDX_HEADS * IDX_DIM), lambda b, t: (b, t, qidx_blk)),
            pl.BlockSpec((1, seq, IDX_DIM), lambda b, t: (b, 0, kidx_blk)),
            pl.BlockSpec((1, tq, LANE), lambda b, t: (b, t, widx_blk)),
            pl.BlockSpec((1, tq, h * KV_LATENT), lambda b, t: (b, t, 0)),
            pl.BlockSpec((1, seq, KV_LATENT), lambda b, t: (b, 0, 0)),
        ],
        out_specs=pl.BlockSpec((1, tq, h * KV_LATENT), lambda b, t: (b, t, 0)),
        out_shape=jax.ShapeDtypeStruct((bsz, seq, h * KV_LATENT), BF16),
        compiler_params=_params("parallel", "parallel"),
        name="dsa_attention",
    )(slopes3, proj, proj, proj, qlat, ckv)


def _dsa_in_weight(w_in):
    sizes = (N_HEADS * HEAD_DIM, KV_LATENT, IDX_HEADS * IDX_DIM, IDX_DIM, IDX_HEADS)
    q, c_kv, q_idx, k_idx, w_idx = jnp.split(w_in, [sum(sizes[:i + 1]) for i in range(4)], axis=1)
    pad = jnp.zeros((w_in.shape[0], _DSA_WIDTH - sum(sizes)), w_in.dtype)
    return jnp.concatenate([q, q_idx, c_kv, k_idx, w_idx, pad], axis=1)


_CFG_IN = {0: (1024, 512, 2048), 1: (1024, 512, 2048), 2: (1024, 1024, 2048), 3: (1024, 1536, 2048)}
_CFG_OUT = {0: (1024, 1024, 2048), 1: (1024, 512, 2048), 2: (512, 2048, 2048), 3: (2048, 512, 2048)}
_CFG_GU = {0: (1024, 512, 2048), 1: (1024, 1408, 2048), 2: (2048, 512, 2048), 3: (512, 512, 2048)}
_CFG_DOWN = {0: (512, 512, 5632), 1: (1024, 512, 5632), 2: (1024, 1024, 2816), 3: (1024, 1024, 1408)}


def _dilated_layer(h, w_in, slopes3, bsz, seq, cfg):
    qkv = _matmul(h, w_in.astype(BF16), BF16, *cfg)
    return _dilated_attention(qkv.reshape(bsz, seq, -1), slopes3, bsz, seq).reshape(bsz * seq, -1)


def _dsa_layer(h, w_in, kv_norm, w_uk, w_uv, slopes3, bsz, seq, topk, cfg=(1024, 512, 2048)):
    m = bsz * seq
    proj = _matmul(h, _dsa_in_weight(w_in).astype(BF16), BF16, *cfg)
    ckv = _rmsnorm(proj, kv_norm, BF16, col_blk=_DSA_COLS["c_kv"] // KV_LATENT)
    qlat = _head_matmul(proj, w_uk.astype(BF16), 0, scale=HEAD_DIM ** -0.5 * LOG2E)
    o_lat = _dsa_attention(
        proj.reshape(bsz, seq, -1), qlat.reshape(bsz, seq, -1), ckv.reshape(bsz, seq, -1), slopes3, bsz, seq, topk
    )
    return _head_matmul(o_lat.reshape(m, -1), w_uv.astype(BF16), 0)


def _forward(x, c, layers, final_norm):
    bsz, seq, d = x.shape
    m = bsz * seq
    topk = min(TOPK_MAX, seq // 4)
    slopes = jnp.exp2(-8.0 * jnp.arange(1, N_HEADS + 1, dtype=F32) / N_HEADS)
    slopes3 = jnp.broadcast_to(slopes[:, None, None], (N_HEADS, 1, LANE))
    xf = x.reshape(m, d)
    for i, (norm_attn, ada_w, ada_b, mixer_w, norm_ffn, w_gate_up, w_down) in enumerate(layers):
        mod3 = _adaln(c, ada_w, ada_b).reshape(bsz * 6, 1, d)
        h = _modulate(xf, norm_attn, mod3, 0, 1, seq)
        if i % 2 == 0:
            w_in, w_out = mixer_w
            o = _dilated_layer(h, w_in, slopes3, bsz, seq, _CFG_IN[i])
        else:
            w_in, kv_norm, w_uk, w_uv, w_out = mixer_w
            o = _dsa_layer(h, w_in, kv_norm, w_uk, w_uv, slopes3, bsz, seq, topk, _CFG_IN[i])
        xf = _matmul_resid(o, w_out.astype(BF16), xf, mod3, 2, seq, *_CFG_OUT[i])
        h = _modulate(xf, norm_ffn, mod3, 3, 4, seq)
        act = _matmul_swiglu(h, w_gate_up.astype(BF16), *_CFG_GU[i])
        xf = _matmul_resid(act, w_down.astype(BF16), xf, mod3, 5, seq, *_CFG_DOWN[i])
    return _rmsnorm(xf, final_norm, x.dtype).reshape(bsz, seq, d)


def kernel(x, c, l0_norm_attn, l0_ada_w, l0_ada_b, l0_w_in, l0_w_out, l0_norm_ffn, l0_w_gate_up, l0_w_down, l1_norm_attn, l1_ada_w, l1_ada_b, l1_w_in, l1_kv_norm, l1_w_uk, l1_w_uv, l1_w_out, l1_norm_ffn, l1_w_gate_up, l1_w_down, l2_norm_attn, l2_ada_w, l2_ada_b, l2_w_in, l2_w_out, l2_norm_ffn, l2_w_gate_up, l2_w_down, l3_norm_attn, l3_ada_w, l3_ada_b, l3_w_in, l3_kv_norm, l3_w_uk, l3_w_uv, l3_w_out, l3_norm_ffn, l3_w_gate_up, l3_w_down, final_norm):
    layers = (
        (l0_norm_attn, l0_ada_w, l0_ada_b, (l0_w_in, l0_w_out), l0_norm_ffn, l0_w_gate_up, l0_w_down),
        (l1_norm_attn, l1_ada_w, l1_ada_b, (l1_w_in, l1_kv_norm, l1_w_uk, l1_w_uv, l1_w_out),
         l1_norm_ffn, l1_w_gate_up, l1_w_down),
        (l2_norm_attn, l2_ada_w, l2_ada_b, (l2_w_in, l2_w_out), l2_norm_ffn, l2_w_gate_up, l2_w_down),
        (l3_norm_attn, l3_ada_w, l3_ada_b, (l3_w_in, l3_kv_norm, l3_w_uk, l3_w_uv, l3_w_out),
         l3_norm_ffn, l3_w_gate_up, l3_w_down),
    )
    return _forward(x, c, layers, final_norm)
```

```python
import functools

import jax
import jax.numpy as jnp
from jax import lax
from jax.experimental import pallas as pl
from jax.experimental.pallas import tpu as pltpu

N_HEADS = 16
HEAD_DIM = 128
DILATED_BRANCHES = ((128, 1), (512, 4), (2048, 16))
BLOCK = 128
KV_LATENT = 256
IDX_HEADS = 16
IDX_DIM = 128
TOPK_MAX = 256
EPS = 1e-6
NEG_BIG = -1e30
LOG2E = 1.4426950408889634

LANE = 128
VMEM_LIMIT = 56 * 1024 * 1024

F32 = jnp.float32
BF16 = jnp.bfloat16

_DSA_COLS = {
    "q": 0,
    "c_kv": N_HEADS * HEAD_DIM,
    "q_idx": N_HEADS * HEAD_DIM + KV_LATENT,
    "k_idx": N_HEADS * HEAD_DIM + KV_LATENT + IDX_HEADS * IDX_DIM,
    "w_idx": N_HEADS * HEAD_DIM + KV_LATENT + IDX_HEADS * IDX_DIM + IDX_DIM,
}
assert _DSA_COLS["c_kv"] % KV_LATENT == 0 and all(v % LANE == 0 for v in _DSA_COLS.values())
_DSA_WIDTH = -(-(_DSA_COLS["w_idx"] + IDX_HEADS) // 512) * 512


def _params(*sem):
    return pltpu.CompilerParams(dimension_semantics=sem, vmem_limit_bytes=VMEM_LIMIT)


def _adaln_kernel(c_ref, w_ref, b_ref, o_ref):
    c = c_ref[...]
    sc = (c * jax.nn.sigmoid(c)).astype(BF16)
    acc = jnp.dot(sc, w_ref[...].astype(BF16), preferred_element_type=F32)
    o_ref[...] = acc + b_ref[...]


def _adaln(c, w, b, tn=1024):
    bsz, d = c.shape
    n = w.shape[1]
    return pl.pallas_call(
        _adaln_kernel,
        grid=(n // tn,),
        in_specs=[
            pl.BlockSpec((bsz, d), lambda j: (0, 0)),
            pl.BlockSpec((d, tn), lambda j: (0, j)),
            pl.BlockSpec((1, tn), lambda j: (0, j)),
        ],
        out_specs=pl.BlockSpec((bsz, tn), lambda j: (0, j)),
        out_shape=jax.ShapeDtypeStruct((bsz, n), F32),
        compiler_params=_params("parallel"),
        name="adaln",
    )(c, w, b.reshape(1, n))


def _modulate_kernel(x_ref, g_ref, shift_ref, scale_ref, o_ref):
    x = x_ref[...]
    ms = jnp.mean(x * x, axis=-1, keepdims=True)
    y = x * lax.rsqrt(ms + EPS) * g_ref[...]
    o_ref[...] = (y * (1.0 + scale_ref[0]) + shift_ref[0]).astype(o_ref.dtype)


def _modulate(x, gain, mod3, shift_k, scale_k, seq, tm=512):
    m, d = x.shape
    per_b = seq // tm
    return pl.pallas_call(
        _modulate_kernel,
        grid=(m // tm,),
        in_specs=[
            pl.BlockSpec((tm, d), lambda i: (i, 0)),
            pl.BlockSpec((1, d), lambda i: (0, 0)),
            pl.BlockSpec((1, 1, d), lambda i: ((i // per_b) * 6 + shift_k, 0, 0)),
            pl.BlockSpec((1, 1, d), lambda i: ((i // per_b) * 6 + scale_k, 0, 0)),
        ],
        out_specs=pl.BlockSpec((tm, d), lambda i: (i, 0)),
        out_shape=jax.ShapeDtypeStruct((m, d), BF16),
        compiler_params=_params("parallel"),
        name="modulate",
    )(x, gain.reshape(1, d), mod3, mod3)


def _rmsnorm_kernel(x_ref, g_ref, o_ref):
    x = x_ref[...].astype(F32)
    ms = jnp.mean(x * x, axis=-1, keepdims=True)
    o_ref[...] = (x * lax.rsqrt(ms + EPS) * g_ref[...]).astype(o_ref.dtype)


def _rmsnorm(x, gain, out_dtype, col_blk=0, tm=512):
    m = x.shape[0]
    d = gain.shape[0]
    return pl.pallas_call(
        _rmsnorm_kernel,
        grid=(m // tm,),
        in_specs=[
            pl.BlockSpec((tm, d), lambda i: (i, col_blk)),
            pl.BlockSpec((1, d), lambda i: (0, 0)),
        ],
        out_specs=pl.BlockSpec((tm, d), lambda i: (i, 0)),
        out_shape=jax.ShapeDtypeStruct((m, d), out_dtype),
        compiler_params=_params("parallel"),
        name="rmsnorm",
    )(x, gain.reshape(1, d))


def _accumulate(acc_refs, products, finish):
    if not acc_refs:
        finish(products())
        return
    k = pl.program_id(2)

    @pl.when(k == 0)
    def _():
        for r in acc_refs:
            r[...] = jnp.zeros_like(r)

    for r, p in zip(acc_refs, products()):
        r[...] += p

    @pl.when(k == pl.num_programs(2) - 1)
    def _():
        finish([r[...] for r in acc_refs])


def _mm_kernel(a_ref, w_ref, o_ref, *acc):
    def products():
        return [jnp.dot(a_ref[...], w_ref[...], preferred_element_type=F32)]

    def finish(v):
        o_ref[...] = v[0].astype(o_ref.dtype)

    _accumulate(acc, products, finish)


def _mm_swiglu_kernel(a_ref, wg_ref, wu_ref, o_ref, *acc):
    def products():
        a = a_ref[...]
        return [jnp.dot(a, wg_ref[...], preferred_element_type=F32),
                jnp.dot(a, wu_ref[...], preferred_element_type=F32)]

    def finish(v):
        g, u = v
        o_ref[...] = (g * jax.nn.sigmoid(g) * u).astype(o_ref.dtype)

    _accumulate(acc, products, finish)


def _mm_resid_kernel(a_ref, w_ref, x_ref, gate_ref, o_ref, *acc):
    def products():
        return [jnp.dot(a_ref[...], w_ref[...], preferred_element_type=F32)]

    def finish(v):
        o_ref[...] = x_ref[...] + gate_ref[0] * v[0]

    _accumulate(acc, products, finish)


def _mm_call(kern, name, a, ws, extra_specs, extra_args, n, out_dtype, tm, tn, tk, w_col_offs):
    m, kd = a.shape
    tk = min(tk, kd)
    nk = kd // tk
    assert m % tm == 0 and n % tn == 0 and kd % tk == 0
    w_specs = [pl.BlockSpec((tk, tn), lambda i, j, k, off=off: (k, j + off)) for off in w_col_offs]
    return pl.pallas_call(
        kern,
        grid=(m // tm, n // tn, nk),
        in_specs=[pl.BlockSpec((tm, tk), lambda i, j, k: (i, k))] + w_specs + extra_specs,
        out_specs=pl.BlockSpec((tm, tn), lambda i, j, k: (i, j)),
        out_shape=jax.ShapeDtypeStruct((m, n), out_dtype),
        scratch_shapes=[pltpu.VMEM((tm, tn), F32) for _ in ws] if nk > 1 else [],
        compiler_params=_params("parallel", "parallel", "arbitrary"),
        name=name,
    )(a, *ws, *extra_args)


def _matmul(a, w, out_dtype, tm=1024, tn=512, tk=2048):
    return _mm_call(_mm_kernel, "matmul", a, [w], [], [], w.shape[1], out_dtype, tm, tn, tk, [0])


def _matmul_swiglu(a, w_gate_up, tm=1024, tn=512, tk=2048):
    f = w_gate_up.shape[1] // 2
    return _mm_call(_mm_swiglu_kernel, "matmul_swiglu", a, [w_gate_up, w_gate_up], [], [], f, BF16,
                    tm, tn, tk, [0, f // tn])


def _matmul_resid(a, w, x, mod3, gate_k, seq, tm=1024, tn=512, tk=2048):
    per_b = seq // tm
    extra_specs = [
        pl.BlockSpec((tm, tn), lambda i, j, k: (i, j)),
        pl.BlockSpec((1, 1, tn), lambda i, j, k: ((i // per_b) * 6 + gate_k, 0, j)),
    ]
    return _mm_call(_mm_resid_kernel, "matmul_resid", a, [w], extra_specs, [x, mod3], w.shape[1], F32,
                    tm, tn, tk, [0])


def _headmm_kernel(a_ref, w_ref, o_ref, *, scale):
    n_heads, din, dout = w_ref.shape
    for h in range(n_heads):
        acc = jnp.dot(a_ref[:, h * din:(h + 1) * din], w_ref[h], preferred_element_type=F32)
        o_ref[:, h * dout:(h + 1) * dout] = (acc * scale).astype(o_ref.dtype)


def _head_matmul(a, w, scale=1.0, tm=512):
    m = a.shape[0]
    h, din, dout = w.shape
    return pl.pallas_call(
        functools.partial(_headmm_kernel, scale=scale),
        grid=(m // tm,),
        in_specs=[
            pl.BlockSpec((tm, h * din), lambda i: (i, 0)),
            pl.BlockSpec((h, din, dout), lambda i: (0, 0, 0)),
        ],
        out_specs=pl.BlockSpec((tm, h * dout), lambda i: (i, 0)),
        out_shape=jax.ShapeDtypeStruct((m, h * dout), BF16),
        compiler_params=_params("parallel"),
        name="head_matmul",
    )(a, w)


def _nt_dot(a, b):
    return lax.dot_general(a, b, (((1,), (1,)), ((), ())), preferred_element_type=F32)


def _dilated_kernel(slope_ref, q_ref, k_ref, v_ref, o_ref, bias_ref, vaug_ref, *, seq, branches):
    n_tiles = seq // BLOCK

    @pl.when(pl.program_id(1) == 0)
    def _():
        slope = slope_ref[0][:, :1]
        qi = lax.broadcasted_iota(jnp.int32, (BLOCK, seq), 0)
        kj = lax.broadcasted_iota(jnp.int32, (BLOCK, seq), 1)
        d = qi + (n_tiles - 1) * BLOCK - kj
        count = jnp.zeros((BLOCK, seq), F32)
        for window, dil in branches:
            assert dil & (dil - 1) == 0
            count = count + jnp.where(((d & (dil - 1)) == 0) & (d <= window), 1.0, 0.0)
        bias = jnp.log2(jnp.maximum(count, 1.0)) - (slope * LOG2E) * d.astype(F32)
        bias_ref[...] = jnp.where((d >= 0) & (count > 0.0), bias, NEG_BIG)
        vaug_ref[:, HEAD_DIM:] = jnp.ones((seq, LANE), BF16)

    vaug_ref[:, :HEAD_DIM] = v_ref[0]

    for n in range(n_tiles):
        n_keys = (n + 1) * BLOCK
        q = (q_ref[0, n * BLOCK:(n + 1) * BLOCK, :].astype(F32) * (HEAD_DIM ** -0.5 * LOG2E)).astype(BF16)
        s = _nt_dot(q, k_ref[0, :n_keys, :]) + bias_ref[:, seq - n_keys:]
        m = jnp.max(s, axis=-1, keepdims=True)
        p = jnp.exp2(s - m).astype(BF16)
        pv = jnp.dot(p, vaug_ref[:n_keys, :], preferred_element_type=F32)
        o_ref[0, n * BLOCK:(n + 1) * BLOCK, :] = (pv[:, :HEAD_DIM] / pv[:, HEAD_DIM:]).astype(o_ref.dtype)


def _dilated_attention(qkv, slopes3, bsz, seq):
    h = N_HEADS
    kern = functools.partial(_dilated_kernel, seq=seq, branches=DILATED_BRANCHES)
    return pl.pallas_call(
        kern,
        grid=(h, bsz),
        in_specs=[
            pl.BlockSpec((1, 1, LANE), lambda j, b: (j, 0, 0)),
            pl.BlockSpec((1, seq, HEAD_DIM), lambda j, b: (b, 0, j)),
            pl.BlockSpec((1, seq, HEAD_DIM), lambda j, b: (b, 0, h + j)),
            pl.BlockSpec((1, seq, HEAD_DIM), lambda j, b: (b, 0, 2 * h + j)),
        ],
        out_specs=pl.BlockSpec((1, seq, HEAD_DIM), lambda j, b: (b, 0, j)),
        out_shape=jax.ShapeDtypeStruct((bsz, seq, h * HEAD_DIM), BF16),
        scratch_shapes=[
            pltpu.VMEM((BLOCK, seq), F32),
            pltpu.VMEM((seq, HEAD_DIM + LANE), BF16),
        ],
        compiler_params=_params("parallel", "arbitrary"),
        name="dilated_attention",
    )(slopes3, qkv, qkv, qkv)


def _count_ge(keys, cand):
    return jnp.sum(jnp.where(keys >= cand, 1.0, 0.0), axis=-1, keepdims=True)


def _stack_heads(ref, col0, width, heads):
    parts = [ref[0, :, col0 + h * width:col0 + (h + 1) * width] for h in heads]
    return parts[0] if len(parts) == 1 else jnp.concatenate(parts, axis=0)


def _dsa_kernel(slope_ref, rows_ref, kidx_ref, qlat_ref, ckv_ref, o_ref, *, seq, topk, tq, n_groups, **static):
    tiles_per_group = seq // tq // n_groups
    group = pl.program_id(1) // tiles_per_group
    for g in range(n_groups):
        body = functools.partial(_dsa_body, slope_ref, rows_ref, kidx_ref, qlat_ref, ckv_ref, o_ref,
                                 seq=(g + 1) * tiles_per_group * tq, topk=topk, tq=tq, **static)
        pl.when(group == g)(body)


def _dsa_body(slope_ref, rows_ref, kidx_ref, qlat_ref, ckv_ref, o_ref, *, seq, topk, tq, qidx_col, widx_col,
              heads_per_dot):
    t = pl.program_id(1)
    q_pos = t * tq + lax.broadcasted_iota(jnp.int32, (tq, seq), 0)
    k_pos = lax.broadcasted_iota(jnp.int32, (tq, seq), 1)
    causal = k_pos <= q_pos

    kidx = kidx_ref[0, :seq, :]
    w_all = rows_ref[0, :, widx_col:widx_col + LANE].astype(F32) * (IDX_DIM ** -0.5 * IDX_HEADS ** -0.5)
    score = jnp.zeros((tq, seq), F32)
    for h0 in range(0, IDX_HEADS, heads_per_dot):
        heads = range(h0, h0 + heads_per_dot)
        logits = _nt_dot(_stack_heads(rows_ref, qidx_col, IDX_DIM, heads), kidx)
        for j, h in enumerate(heads):
            score = score + w_all[:, h:h + 1] * jnp.maximum(logits[j * tq:(j + 1) * tq], 0.0)
    score = jnp.where(causal, score, NEG_BIG)
    score = jnp.where(score == 0.0, 0.0, score)

    bits = lax.bitcast_convert_type(score, jnp.int32)
    keys = jnp.where(bits < 0, bits ^ jnp.int32(0x7FFFFFFF), bits)

    kf = float(topk)
    int_min = jnp.int32(-(2 ** 31))
    keys_t = keys.T
    n_acc = 8

    def count_ge_t(cand):
        hit = jnp.where(keys_t >= cand, 1.0, 0.0).reshape(n_acc, seq // (8 * n_acc), 8, tq)
        partial = jnp.sum(jnp.sum(hit, axis=1), axis=0)
        return jnp.sum(partial, axis=0, keepdims=True)

    thr_t = jnp.where(count_ge_t(jnp.int32(0)) >= kf, jnp.int32(0), int_min)

    def bit_step(i, thr_t):
        cand = thr_t | (jnp.int32(1) << (30 - i))
        return jnp.where(count_ge_t(cand) >= kf, cand, thr_t)

    thr_t = lax.fori_loop(0, 31, bit_step, thr_t)
    thr = jnp.broadcast_to(thr_t, (tq, tq)).T[:, :1]

    gt = keys > thr
    eq = keys == thr
    n_gt = jnp.sum(jnp.where(gt, 1.0, 0.0), axis=-1, keepdims=True)
    n_eq = jnp.sum(jnp.where(eq, 1.0, 0.0), axis=-1, keepdims=True)
    need = kf - n_gt
    tie_rows = jnp.max(jnp.where(n_eq > need, 1.0, 0.0))

    def tie_cut():
        def cut_step(i, cut):
            cand = cut - (jnp.int32(1) << (n_bits - 1 - i))
            cnt = jnp.sum(jnp.where(eq & (k_pos <= cand), 1.0, 0.0), axis=-1, keepdims=True)
            return jnp.where(cnt >= need, cand, cut)

        n_bits = (seq - 1).bit_length()
        return lax.fori_loop(0, n_bits, cut_step, jnp.full((tq, 1), (1 << n_bits) - 1, jnp.int32))

    cut = lax.cond(tie_rows > 0.0, tie_cut, lambda: jnp.full((tq, 1), seq, jnp.int32))
    selected = (gt | (eq & (k_pos <= cut))) & causal
    mask_bias = jnp.where(selected, 0.0, NEG_BIG)

    k_pos_row = lax.broadcasted_iota(jnp.int32, (1, seq), 1).astype(F32)
    ckv = ckv_ref[0, :seq, :]
    for h0 in range(0, N_HEADS, heads_per_dot):
        heads = range(h0, h0 + heads_per_dot)
        s_all = _nt_dot(_stack_heads(qlat_ref, 0, KV_LATENT, heads), ckv)
        probs, sums = [], []
        for j, h in enumerate(heads):
            slope = slope_ref[h][:, :1]
            s = s_all[j * tq:(j + 1) * tq] + (slope * LOG2E) * k_pos_row + mask_bias
            p = jnp.exp2(s - jnp.max(s, axis=-1, keepdims=True))
            sums.append(jnp.sum(p, axis=-1, keepdims=True))
            probs.append(p.astype(BF16))
        p_all = probs[0] if len(probs) == 1 else jnp.concatenate(probs, axis=0)
        o_all = jnp.dot(p_all, ckv, preferred_element_type=F32)
        for j, h in enumerate(heads):
            o = o_all[j * tq:(j + 1) * tq] / sums[j]
            o_ref[0, :, h * KV_LATENT:(h + 1) * KV_LATENT] = o.astype(o_ref.dtype)


def _dsa_attention(proj, qlat, ckv, slopes3, bsz, seq, topk, tq=128, n_groups=4, heads_per_dot=2):
    h = N_HEADS
    width = proj.shape[-1]
    kern = functools.partial(_dsa_kernel, seq=seq, topk=topk, tq=tq, n_groups=n_groups,
                             qidx_col=_DSA_COLS["q_idx"], widx_col=_DSA_COLS["w_idx"],
                             heads_per_dot=heads_per_dot)
    return pl.pallas_call(
        kern,
        grid=(bsz, seq // tq),
        in_specs=[
            pl.BlockSpec((h, 1, LANE), lambda b, t: (0, 0, 0)),
            pl.BlockSpec((1, tq, width), lambda b, t: (b, t, 0)),
            pl.BlockSpec((1, seq, IDX_DIM), lambda b, t: (b, 0, _DSA_COLS["k_idx"] // IDX_DIM)),
            pl.BlockSpec((1, tq, h * KV_LATENT), lambda b, t: (b, t, 0)),
            pl.BlockSpec((1, seq, KV_LATENT), lambda b, t: (b, 0, 0)),
        ],
        out_specs=pl.BlockSpec((1, tq, h * KV_LATENT), lambda b, t: (b, t, 0)),
        out_shape=jax.ShapeDtypeStruct((bsz, seq, h * KV_LATENT), BF16),
        compiler_params=_params("parallel", "parallel"),
        name="dsa_attention",
    )(slopes3, proj, proj, qlat, ckv)


def _dsa_in_weight(w_in):
    return jnp.pad(w_in.astype(BF16), ((0, 0), (0, _DSA_WIDTH - w_in.shape[1])))


_CFG_IN = {0: (1024, 1536, 2048), 1: (1024, 1536, 2048), 2: (1024, 3072, 2048), 3: (1024, 2304, 2048)}
_CFG_OUT = {0: (512, 2048, 2048), 1: (256, 2048, 2048), 2: (512, 2048, 2048), 3: (512, 1024, 2048)}
_CFG_GU = {0: (1024, 512, 2048), 1: (512, 1408, 2048), 2: (1024, 512, 2048), 3: (1024, 512, 2048)}
_CFG_DOWN = {0: (1024, 512, 5632), 1: (512, 2048, 2816), 2: (512, 2048, 1408), 3: (512, 1024, 5632)}


def _dilated_layer(h, w_in, slopes3, bsz, seq, cfg):
    qkv = _matmul(h, w_in.astype(BF16), BF16, *cfg)
    return _dilated_attention(qkv.reshape(bsz, seq, -1), slopes3, bsz, seq).reshape(bsz * seq, -1)


def _dsa_layer(h, w_in, kv_norm, w_uk, w_uv, slopes3, bsz, seq, topk, cfg=(1024, 512, 2048)):
    m = bsz * seq
    proj = _matmul(h, _dsa_in_weight(w_in), BF16, *cfg)
    ckv = _rmsnorm(proj, kv_norm, BF16, col_blk=_DSA_COLS["c_kv"] // KV_LATENT)
    qlat = _head_matmul(proj, w_uk.astype(BF16), scale=HEAD_DIM ** -0.5 * LOG2E)
    o_lat = _dsa_attention(
        proj.reshape(bsz, seq, -1), qlat.reshape(bsz, seq, -1), ckv.reshape(bsz, seq, -1), slopes3, bsz, seq, topk
    )
    return _head_matmul(o_lat.reshape(m, -1), w_uv.astype(BF16))


def _forward(x, c, layers, final_norm):
    bsz, seq, d = x.shape
    m = bsz * seq
    topk = min(TOPK_MAX, seq // 4)
    slopes = jnp.exp2(-8.0 * jnp.arange(1, N_HEADS + 1, dtype=F32) / N_HEADS)
    slopes3 = jnp.broadcast_to(slopes[:, None, None], (N_HEADS, 1, LANE))
    xf = x.reshape(m, d)
    for i, (norm_attn, ada_w, ada_b, mixer_w, norm_ffn, w_gate_up, w_down) in enumerate(layers):
        mod3 = _adaln(c, ada_w, ada_b).reshape(bsz * 6, 1, d)
        h = _modulate(xf, norm_attn, mod3, 0, 1, seq)
        if i % 2 == 0:
            w_in, w_out = mixer_w
            o = _dilated_layer(h, w_in, slopes3, bsz, seq, _CFG_IN[i])
        else:
            w_in, kv_norm, w_uk, w_uv, w_out = mixer_w
            o = _dsa_layer(h, w_in, kv_norm, w_uk, w_uv, slopes3, bsz, seq, topk, _CFG_IN[i])
        xf = _matmul_resid(o, w_out.astype(BF16), xf, mod3, 2, seq, *_CFG_OUT[i])
        h = _modulate(xf, norm_ffn, mod3, 3, 4, seq)
        act = _matmul_swiglu(h, w_gate_up.astype(BF16), *_CFG_GU[i])
        xf = _matmul_resid(act, w_down.astype(BF16), xf, mod3, 5, seq, *_CFG_DOWN[i])
    return _rmsnorm(xf, final_norm, x.dtype).reshape(bsz, seq, d)


def kernel(x, c, l0_norm_attn, l0_ada_w, l0_ada_b, l0_w_in, l0_w_out, l0_norm_ffn, l0_w_gate_up, l0_w_down, l1_norm_attn, l1_ada_w, l1_ada_b, l1_w_in, l1_kv_norm, l1_w_uk, l1_w_uv, l1_w_out, l1_norm_ffn, l1_w_gate_up, l1_w_down, l2_norm_attn, l2_ada_w, l2_ada_b, l2_w_in, l2_w_out, l2_norm_ffn, l2_w_gate_up, l2_w_down, l3_norm_attn, l3_ada_w, l3_ada_b, l3_w_in, l3_kv_norm, l3_w_uk, l3_w_uv, l3_w_out, l3_norm_ffn, l3_w_gate_up, l3_w_down, final_norm):
    layers = (
        (l0_norm_attn, l0_ada_w, l0_ada_b, (l0_w_in, l0_w_out), l0_norm_ffn, l0_w_gate_up, l0_w_down),
        (l1_norm_attn, l1_ada_w, l1_ada_b, (l1_w_in, l1_kv_norm, l1_w_uk, l1_w_uv, l1_w_out),
         l1_norm_ffn, l1_w_gate_up, l1_w_down),
        (l2_norm_attn, l2_ada_w, l2_ada_b, (l2_w_in, l2_w_out), l2_norm_ffn, l2_w_gate_up, l2_w_down),
        (l3_norm_attn, l3_ada_w, l3_ada_b, (l3_w_in, l3_kv_norm, l3_w_uk, l3_w_uv, l3_w_out),
         l3_norm_ffn, l3_w_gate_up, l3_w_down),
    )
    return _forward(x, c, layers, final_norm)
```

```python
import functools

import jax
import jax.numpy as jnp
from jax import lax
from jax.experimental import pallas as pl
from jax.experimental.pallas import tpu as pltpu

N_HEADS = 16
HEAD_DIM = 128
DILATED_BRANCHES = ((128, 1), (512, 4), (2048, 16))
BLOCK = 128
KV_LATENT = 256
IDX_HEADS = 16
IDX_DIM = 128
TOPK_MAX = 256
EPS = 1e-6
NEG_BIG = -1e30
LOG2E = 1.4426950408889634

LANE = 128
VMEM_LIMIT = 56 * 1024 * 1024

F32 = jnp.float32
BF16 = jnp.bfloat16

_DSA_COLS = {
    "q": 0,
    "c_kv": N_HEADS * HEAD_DIM,
    "q_idx": N_HEADS * HEAD_DIM + KV_LATENT,
    "k_idx": N_HEADS * HEAD_DIM + KV_LATENT + IDX_HEADS * IDX_DIM,
    "w_idx": N_HEADS * HEAD_DIM + KV_LATENT + IDX_HEADS * IDX_DIM + IDX_DIM,
}
assert _DSA_COLS["c_kv"] % KV_LATENT == 0 and all(v % LANE == 0 for v in _DSA_COLS.values())
_DSA_WIDTH = -(-(_DSA_COLS["w_idx"] + IDX_HEADS) // 512) * 512


def _params(*sem):
    return pltpu.CompilerParams(dimension_semantics=sem, vmem_limit_bytes=VMEM_LIMIT)


def _adaln_kernel(c_ref, w_ref, b_ref, o_ref):
    c = c_ref[...]
    sc = (c * jax.nn.sigmoid(c)).astype(BF16)
    acc = jnp.dot(sc, w_ref[...].astype(BF16), preferred_element_type=F32)
    o_ref[...] = acc + b_ref[...]


def _adaln(c, w, b, tn=1024):
    bsz, d = c.shape
    n = w.shape[1]
    return pl.pallas_call(
        _adaln_kernel,
        grid=(n // tn,),
        in_specs=[
            pl.BlockSpec((bsz, d), lambda j: (0, 0)),
            pl.BlockSpec((d, tn), lambda j: (0, j)),
            pl.BlockSpec((1, tn), lambda j: (0, j)),
        ],
        out_specs=pl.BlockSpec((bsz, tn), lambda j: (0, j)),
        out_shape=jax.ShapeDtypeStruct((bsz, n), F32),
        compiler_params=_params("parallel"),
        name="adaln",
    )(c, w, b.reshape(1, n))


def _modulate_into(h_ref, x_ref, g_ref, shift_ref, scale_ref):
    @pl.when(pl.program_id(1) == 0)
    def _():
        x = x_ref[...]
        ms = jnp.mean(x * x, axis=-1, keepdims=True)
        y = x * lax.rsqrt(ms + EPS) * g_ref[...]
        h_ref[...] = (y * (1.0 + scale_ref[0]) + shift_ref[0]).astype(h_ref.dtype)


def _mod_mm_kernel(x_ref, g_ref, shift_ref, scale_ref, w_ref, o_ref, h_ref):
    _modulate_into(h_ref, x_ref, g_ref, shift_ref, scale_ref)
    o_ref[...] = jnp.dot(h_ref[...], w_ref[...], preferred_element_type=F32).astype(o_ref.dtype)


def _mod_mm_swiglu_kernel(x_ref, g_ref, shift_ref, scale_ref, wg_ref, wu_ref, o_ref, h_ref):
    _modulate_into(h_ref, x_ref, g_ref, shift_ref, scale_ref)
    h = h_ref[...]
    g = jnp.dot(h, wg_ref[...], preferred_element_type=F32)
    u = jnp.dot(h, wu_ref[...], preferred_element_type=F32)
    o_ref[...] = (g * jax.nn.sigmoid(g) * u).astype(o_ref.dtype)


def _mod_mm_call(kern, name, x, gain, mod3, shift_k, scale_k, seq, ws, w_col_offs, n, tm, tn):
    m, d = x.shape
    assert m % tm == 0 and n % tn == 0 and seq % tm == 0
    per_b = seq // tm
    w_specs = [pl.BlockSpec((d, tn), lambda i, j, off=off: (0, j + off)) for off in w_col_offs]
    return pl.pallas_call(
        kern,
        grid=(m // tm, n // tn),
        in_specs=[
            pl.BlockSpec((tm, d), lambda i, j: (i, 0)),
            pl.BlockSpec((1, d), lambda i, j: (0, 0)),
            pl.BlockSpec((1, 1, d), lambda i, j: ((i // per_b) * 6 + shift_k, 0, 0)),
            pl.BlockSpec((1, 1, d), lambda i, j: ((i // per_b) * 6 + scale_k, 0, 0)),
        ] + w_specs,
        out_specs=pl.BlockSpec((tm, tn), lambda i, j: (i, j)),
        out_shape=jax.ShapeDtypeStruct((m, n), BF16),
        scratch_shapes=[pltpu.VMEM((tm, d), BF16)],
        compiler_params=_params("parallel", "arbitrary"),
        name=name,
    )(x, gain.reshape(1, d), mod3, mod3, *ws)


def _mod_matmul(x, gain, mod3, shift_k, scale_k, seq, w, tm, tn):
    return _mod_mm_call(_mod_mm_kernel, "mod_matmul", x, gain, mod3, shift_k, scale_k, seq, [w], [0],
                        w.shape[1], tm, tn)


def _mod_matmul_swiglu(x, gain, mod3, shift_k, scale_k, seq, w_gate_up, tm, tn):
    f = w_gate_up.shape[1] // 2
    return _mod_mm_call(_mod_mm_swiglu_kernel, "mod_matmul_swiglu", x, gain, mod3, shift_k, scale_k, seq,
                        [w_gate_up, w_gate_up], [0, f // tn], f, tm, tn)


def _rmsnorm_kernel(x_ref, g_ref, o_ref):
    x = x_ref[...].astype(F32)
    ms = jnp.mean(x * x, axis=-1, keepdims=True)
    o_ref[...] = (x * lax.rsqrt(ms + EPS) * g_ref[...]).astype(o_ref.dtype)


def _rmsnorm(x, gain, out_dtype, col_blk=0, tm=512):
    m = x.shape[0]
    d = gain.shape[0]
    return pl.pallas_call(
        _rmsnorm_kernel,
        grid=(m // tm,),
        in_specs=[
            pl.BlockSpec((tm, d), lambda i: (i, col_blk)),
            pl.BlockSpec((1, d), lambda i: (0, 0)),
        ],
        out_specs=pl.BlockSpec((tm, d), lambda i: (i, 0)),
        out_shape=jax.ShapeDtypeStruct((m, d), out_dtype),
        compiler_params=_params("parallel"),
        name="rmsnorm",
    )(x, gain.reshape(1, d))


def _mm_resid_kernel(a_ref, w_ref, x_ref, gate_ref, o_ref):
    acc = jnp.dot(a_ref[...], w_ref[...], preferred_element_type=F32)
    o_ref[...] = x_ref[...] + gate_ref[0] * acc


def _matmul_resid(a, w, x, mod3, gate_k, seq, tm, tn):
    m, kd = a.shape
    n = w.shape[1]
    assert m % tm == 0 and n % tn == 0 and seq % tm == 0
    per_b = seq // tm
    return pl.pallas_call(
        _mm_resid_kernel,
        grid=(m // tm, n // tn),
        in_specs=[
            pl.BlockSpec((tm, kd), lambda i, j: (i, 0)),
            pl.BlockSpec((kd, tn), lambda i, j: (0, j)),
            pl.BlockSpec((tm, tn), lambda i, j: (i, j)),
            pl.BlockSpec((1, 1, tn), lambda i, j: ((i // per_b) * 6 + gate_k, 0, j)),
        ],
        out_specs=pl.BlockSpec((tm, tn), lambda i, j: (i, j)),
        out_shape=jax.ShapeDtypeStruct((m, n), F32),
        compiler_params=_params("parallel", "parallel"),
        name="matmul_resid",
    )(a, w, x, mod3)


def _headmm_kernel(a_ref, w_ref, o_ref, *, scale):
    n_heads, din, dout = w_ref.shape
    for h in range(n_heads):
        acc = jnp.dot(a_ref[:, h * din:(h + 1) * din], w_ref[h], preferred_element_type=F32)
        o_ref[:, h * dout:(h + 1) * dout] = (acc * scale).astype(o_ref.dtype)


def _head_matmul(a, w, scale=1.0, tm=512):
    m = a.shape[0]
    h, din, dout = w.shape
    return pl.pallas_call(
        functools.partial(_headmm_kernel, scale=scale),
        grid=(m // tm,),
        in_specs=[
            pl.BlockSpec((tm, h * din), lambda i: (i, 0)),
            pl.BlockSpec((h, din, dout), lambda i: (0, 0, 0)),
        ],
        out_specs=pl.BlockSpec((tm, h * dout), lambda i: (i, 0)),
        out_shape=jax.ShapeDtypeStruct((m, h * dout), BF16),
        compiler_params=_params("parallel"),
        name="head_matmul",
    )(a, w)


def _nt_dot(a, b):
    return lax.dot_general(a, b, (((1,), (1,)), ((), ())), preferred_element_type=F32)


def _dilated_kernel(slope_ref, q_ref, k_ref, v_ref, o_ref, bias_ref, vaug_ref, *, seq, branches, tq):
    n_tiles = seq // tq

    @pl.when(pl.program_id(1) == 0)
    def _():
        slope = slope_ref[0][:, :1]
        qi = lax.broadcasted_iota(jnp.int32, (tq, seq), 0)
        kj = lax.broadcasted_iota(jnp.int32, (tq, seq), 1)
        d = qi + (seq - tq) - kj
        count = jnp.zeros((tq, seq), F32)
        for window, dil in branches:
            assert dil & (dil - 1) == 0
            count = count + jnp.where(((d & (dil - 1)) == 0) & (d <= window), 1.0, 0.0)
        bias = jnp.log2(jnp.maximum(count, 1.0)) - (slope * LOG2E) * d.astype(F32)
        bias_ref[...] = jnp.where((d >= 0) & (count > 0.0), bias, NEG_BIG)
        vaug_ref[:, HEAD_DIM:] = jnp.ones((seq, LANE), BF16)

    vaug_ref[:, :HEAD_DIM] = v_ref[0]

    for n in range(n_tiles):
        n_keys = (n + 1) * tq
        q = (q_ref[0, n * tq:(n + 1) * tq, :].astype(F32) * (HEAD_DIM ** -0.5 * LOG2E)).astype(BF16)
        s = _nt_dot(q, k_ref[0, :n_keys, :]) + bias_ref[:, seq - n_keys:]
        m = jnp.max(s, axis=-1, keepdims=True)
        p = jnp.exp2(s - m).astype(BF16)
        pv = jnp.dot(p, vaug_ref[:n_keys, :], preferred_element_type=F32)
        o_ref[0, n * tq:(n + 1) * tq, :] = (pv[:, :HEAD_DIM] / pv[:, HEAD_DIM:]).astype(o_ref.dtype)


def _dilated_attention(qkv, slopes3, bsz, seq, tq=256):
    h = N_HEADS
    assert seq % tq == 0
    kern = functools.partial(_dilated_kernel, seq=seq, branches=DILATED_BRANCHES, tq=tq)
    return pl.pallas_call(
        kern,
        grid=(h, bsz),
        in_specs=[
            pl.BlockSpec((1, 1, LANE), lambda j, b: (j, 0, 0)),
            pl.BlockSpec((1, seq, HEAD_DIM), lambda j, b: (b, 0, j)),
            pl.BlockSpec((1, seq, HEAD_DIM), lambda j, b: (b, 0, h + j)),
            pl.BlockSpec((1, seq, HEAD_DIM), lambda j, b: (b, 0, 2 * h + j)),
        ],
        out_specs=pl.BlockSpec((1, seq, HEAD_DIM), lambda j, b: (b, 0, j)),
        out_shape=jax.ShapeDtypeStruct((bsz, seq, h * HEAD_DIM), BF16),
        scratch_shapes=[
            pltpu.VMEM((tq, seq), F32),
            pltpu.VMEM((seq, HEAD_DIM + LANE), BF16),
        ],
        compiler_params=_params("parallel", "arbitrary"),
        name="dilated_attention",
    )(slopes3, qkv, qkv, qkv)


def _count_ge(keys, cand):
    return jnp.sum(jnp.where(keys >= cand, 1.0, 0.0), axis=-1, keepdims=True)


def _stack_heads(ref, col0, width, heads):
    parts = [ref[0, :, col0 + h * width:col0 + (h + 1) * width] for h in heads]
    return parts[0] if len(parts) == 1 else jnp.concatenate(parts, axis=0)


def _dsa_kernel(slope_ref, rows_ref, kidx_ref, qlat_ref, ckv_ref, o_ref, *, seq, topk, tq, n_groups, **static):
    tiles_per_group = seq // tq // n_groups
    group = pl.program_id(1) // tiles_per_group
    for g in range(n_groups):
        body = functools.partial(_dsa_body, slope_ref, rows_ref, kidx_ref, qlat_ref, ckv_ref, o_ref,
                                 seq=(g + 1) * tiles_per_group * tq, topk=topk, tq=tq, **static)
        pl.when(group == g)(body)


def _dsa_body(slope_ref, rows_ref, kidx_ref, qlat_ref, ckv_ref, o_ref, *, seq, topk, tq, qidx_col, widx_col,
              idx_heads_per_dot, att_heads_per_dot):
    t = pl.program_id(1)
    q_pos = t * tq + lax.broadcasted_iota(jnp.int32, (tq, seq), 0)
    k_pos = lax.broadcasted_iota(jnp.int32, (tq, seq), 1)
    causal = k_pos <= q_pos

    kidx = kidx_ref[0, :seq, :]
    w_all = rows_ref[0, :, widx_col:widx_col + LANE].astype(F32) * (IDX_DIM ** -0.5 * IDX_HEADS ** -0.5)
    score = jnp.zeros((tq, seq), F32)
    for h0 in range(0, IDX_HEADS, idx_heads_per_dot):
        heads = range(h0, h0 + idx_heads_per_dot)
        logits = _nt_dot(_stack_heads(rows_ref, qidx_col, IDX_DIM, heads), kidx)
        for j, h in enumerate(heads):
            score = score + w_all[:, h:h + 1] * jnp.maximum(logits[j * tq:(j + 1) * tq], 0.0)
    score = jnp.where(causal, score, NEG_BIG)
    score = jnp.where(score == 0.0, 0.0, score)

    bits = lax.bitcast_convert_type(score, jnp.int32)
    keys = jnp.where(bits < 0, bits ^ jnp.int32(0x7FFFFFFF), bits)

    kf = float(topk)
    int_min = jnp.int32(-(2 ** 31))
    keys_t = keys.T
    n_acc = 8

    def count_ge_t(cand):
        hit = jnp.where(keys_t >= cand, 1.0, 0.0).reshape(n_acc, seq // (8 * n_acc), 8, tq)
        partial = jnp.sum(jnp.sum(hit, axis=1), axis=0)
        return jnp.sum(partial, axis=0, keepdims=True)

    thr_t = jnp.where(count_ge_t(jnp.int32(0)) >= kf, jnp.int32(0), int_min)

    def bit_step(i, thr_t):
        cand = thr_t | (jnp.int32(1) << (30 - i))
        return jnp.where(count_ge_t(cand) >= kf, cand, thr_t)

    thr_t = lax.fori_loop(0, 31, bit_step, thr_t)
    thr = jnp.broadcast_to(thr_t, (tq, tq)).T[:, :1]

    gt = keys > thr
    eq = keys == thr
    n_gt = jnp.sum(jnp.where(gt, 1.0, 0.0), axis=-1, keepdims=True)
    n_eq = jnp.sum(jnp.where(eq, 1.0, 0.0), axis=-1, keepdims=True)
    need = kf - n_gt
    tie_rows = jnp.max(jnp.where(n_eq > need, 1.0, 0.0))

    def tie_cut():
        def cut_step(i, cut):
            cand = cut - (jnp.int32(1) << (n_bits - 1 - i))
            cnt = jnp.sum(jnp.where(eq & (k_pos <= cand), 1.0, 0.0), axis=-1, keepdims=True)
            return jnp.where(cnt >= need, cand, cut)

        n_bits = (seq - 1).bit_length()
        return lax.fori_loop(0, n_bits, cut_step, jnp.full((tq, 1), (1 << n_bits) - 1, jnp.int32))

    cut = lax.cond(tie_rows > 0.0, tie_cut, lambda: jnp.full((tq, 1), seq, jnp.int32))
    selected = (gt | (eq & (k_pos <= cut))) & causal
    mask_bias = jnp.where(selected, 0.0, NEG_BIG)

    k_pos_row = lax.broadcasted_iota(jnp.int32, (1, seq), 1).astype(F32)
    ckv = ckv_ref[0, :seq, :]
    for h0 in range(0, N_HEADS, att_heads_per_dot):
        heads = range(h0, h0 + att_heads_per_dot)
        s_all = _nt_dot(_stack_heads(qlat_ref, 0, KV_LATENT, heads), ckv)
        probs, sums = [], []
        for j, h in enumerate(heads):
            slope = slope_ref[h][:, :1]
            s = s_all[j * tq:(j + 1) * tq] + (slope * LOG2E) * k_pos_row + mask_bias
            p = jnp.exp2(s - jnp.max(s, axis=-1, keepdims=True))
            sums.append(jnp.sum(p, axis=-1, keepdims=True))
            probs.append(p.astype(BF16))
        p_all = probs[0] if len(probs) == 1 else jnp.concatenate(probs, axis=0)
        o_all = jnp.dot(p_all, ckv, preferred_element_type=F32)
        for j, h in enumerate(heads):
            o = o_all[j * tq:(j + 1) * tq] / sums[j]
            o_ref[0, :, h * KV_LATENT:(h + 1) * KV_LATENT] = o.astype(o_ref.dtype)


def _dsa_attention(proj, qlat, ckv, slopes3, bsz, seq, topk, tq=128, n_groups=4, idx_heads_per_dot=2,
                   att_heads_per_dot=2):
    h = N_HEADS
    width = proj.shape[-1]
    kern = functools.partial(_dsa_kernel, seq=seq, topk=topk, tq=tq, n_groups=n_groups,
                             qidx_col=_DSA_COLS["q_idx"], widx_col=_DSA_COLS["w_idx"],
                             idx_heads_per_dot=idx_heads_per_dot, att_heads_per_dot=att_heads_per_dot)
    return pl.pallas_call(
        kern,
        grid=(bsz, seq // tq),
        in_specs=[
            pl.BlockSpec((h, 1, LANE), lambda b, t: (0, 0, 0)),
            pl.BlockSpec((1, tq, width), lambda b, t: (b, t, 0)),
            pl.BlockSpec((1, seq, IDX_DIM), lambda b, t: (b, 0, _DSA_COLS["k_idx"] // IDX_DIM)),
            pl.BlockSpec((1, tq, h * KV_LATENT), lambda b, t: (b, t, 0)),
            pl.BlockSpec((1, seq, KV_LATENT), lambda b, t: (b, 0, 0)),
        ],
        out_specs=pl.BlockSpec((1, tq, h * KV_LATENT), lambda b, t: (b, t, 0)),
        out_shape=jax.ShapeDtypeStruct((bsz, seq, h * KV_LATENT), BF16),
        compiler_params=_params("parallel", "parallel"),
        name="dsa_attention",
    )(slopes3, proj, proj, qlat, ckv)


def _dsa_in_weight(w_in):
    return jnp.pad(w_in.astype(BF16), ((0, 0), (0, _DSA_WIDTH - w_in.shape[1])))


_TILE_IN = (1024, 1536)
_TILE_GATE_UP = (1024, 512)
_TILE_OUT = (512, 2048)
_TILE_DOWN = (1024, 512)


def _dilated_layer(proj, slopes3, bsz, seq):
    return _dilated_attention(proj.reshape(bsz, seq, -1), slopes3, bsz, seq).reshape(bsz * seq, -1)


def _dsa_layer(proj, kv_norm, w_uk, w_uv, slopes3, bsz, seq, topk):
    m = bsz * seq
    ckv = _rmsnorm(proj, kv_norm, BF16, col_blk=_DSA_COLS["c_kv"] // KV_LATENT)
    qlat = _head_matmul(proj, w_uk.astype(BF16), scale=HEAD_DIM ** -0.5 * LOG2E)
    o_lat = _dsa_attention(
        proj.reshape(bsz, seq, -1), qlat.reshape(bsz, seq, -1), ckv.reshape(bsz, seq, -1), slopes3, bsz, seq, topk
    )
    return _head_matmul(o_lat.reshape(m, -1), w_uv.astype(BF16))


def _forward(x, c, layers, final_norm):
    bsz, seq, d = x.shape
    m = bsz * seq
    topk = min(TOPK_MAX, seq // 4)
    slopes = jnp.exp2(-8.0 * jnp.arange(1, N_HEADS + 1, dtype=F32) / N_HEADS)
    slopes3 = jnp.broadcast_to(slopes[:, None, None], (N_HEADS, 1, LANE))
    xf = x.reshape(m, d)
    for i, (norm_attn, ada_w, ada_b, mixer_w, norm_ffn, w_gate_up, w_down) in enumerate(layers):
        mod3 = _adaln(c, ada_w, ada_b).reshape(bsz * 6, 1, d)
        if i % 2 == 0:
            w_in, w_out = mixer_w
            proj = _mod_matmul(xf, norm_attn, mod3, 0, 1, seq, w_in.astype(BF16), *_TILE_IN)
            o = _dilated_layer(proj, slopes3, bsz, seq)
        else:
            w_in, kv_norm, w_uk, w_uv, w_out = mixer_w
            proj = _mod_matmul(xf, norm_attn, mod3, 0, 1, seq, _dsa_in_weight(w_in), *_TILE_IN)
            o = _dsa_layer(proj, kv_norm, w_uk, w_uv, slopes3, bsz, seq, topk)
        xf = _matmul_resid(o, w_out.astype(BF16), xf, mod3, 2, seq, *_TILE_OUT)
        act = _mod_matmul_swiglu(xf, norm_ffn, mod3, 3, 4, seq, w_gate_up.astype(BF16), *_TILE_GATE_UP)
        xf = _matmul_resid(act, w_down.astype(BF16), xf, mod3, 5, seq, *_TILE_DOWN)
    return _rmsnorm(xf, final_norm, x.dtype).reshape(bsz, seq, d)


def kernel(x, c, l0_norm_attn, l0_ada_w, l0_ada_b, l0_w_in, l0_w_out, l0_norm_ffn, l0_w_gate_up, l0_w_down, l1_norm_attn, l1_ada_w, l1_ada_b, l1_w_in, l1_kv_norm, l1_w_uk, l1_w_uv, l1_w_out, l1_norm_ffn, l1_w_gate_up, l1_w_down, l2_norm_attn, l2_ada_w, l2_ada_b, l2_w_in, l2_w_out, l2_norm_ffn, l2_w_gate_up, l2_w_down, l3_norm_attn, l3_ada_w, l3_ada_b, l3_w_in, l3_kv_norm, l3_w_uk, l3_w_uv, l3_w_out, l3_norm_ffn, l3_w_gate_up, l3_w_down, final_norm):
    layers = (
        (l0_norm_attn, l0_ada_w, l0_ada_b, (l0_w_in, l0_w_out), l0_norm_ffn, l0_w_gate_up, l0_w_down),
        (l1_norm_attn, l1_ada_w, l1_ada_b, (l1_w_in, l1_kv_norm, l1_w_uk, l1_w_uv, l1_w_out),
         l1_norm_ffn, l1_w_gate_up, l1_w_down),
        (l2_norm_attn, l2_ada_w, l2_ada_b, (l2_w_in, l2_w_out), l2_norm_ffn, l2_w_gate_up, l2_w_down),
        (l3_norm_attn, l3_ada_w, l3_ada_b, (l3_w_in, l3_kv_norm, l3_w_uk, l3_w_uv, l3_w_out),
         l3_norm_ffn, l3_w_gate_up, l3_w_down),
    )
    return _forward(x, c, layers, final_norm)
```

```python
import functools

import jax
import jax.numpy as jnp
from jax import lax
from jax.experimental import pallas as pl
from jax.experimental.pallas import tpu as pltpu

N_HEADS = 16
HEAD_DIM = 128
DILATED_BRANCHES = ((128, 1), (512, 4), (2048, 16))
BLOCK = 128
KV_LATENT = 256
IDX_HEADS = 16
IDX_DIM = 128
TOPK_MAX = 256
EPS = 1e-6
NEG_BIG = -1e30
LOG2E = 1.4426950408889634

LANE = 128
BF16_ROWS = 16
VMEM_LIMIT = 56 * 1024 * 1024

F32 = jnp.float32
BF16 = jnp.bfloat16

_DSA_COLS = {
    "q": 0,
    "c_kv": N_HEADS * HEAD_DIM,
    "q_idx": N_HEADS * HEAD_DIM + KV_LATENT,
    "k_idx": N_HEADS * HEAD_DIM + KV_LATENT + IDX_HEADS * IDX_DIM,
    "w_idx": N_HEADS * HEAD_DIM + KV_LATENT + IDX_HEADS * IDX_DIM + IDX_DIM,
}
assert _DSA_COLS["c_kv"] % KV_LATENT == 0 and all(v % LANE == 0 for v in _DSA_COLS.values())
_DSA_WIDTH = -(-(_DSA_COLS["w_idx"] + IDX_HEADS) // 512) * 512


def _params(*sem):
    return pltpu.CompilerParams(dimension_semantics=sem, vmem_limit_bytes=VMEM_LIMIT)


def _adaln_kernel(c_ref, w_ref, b_ref, o_ref):
    c = c_ref[...]
    sc = (c * jax.nn.sigmoid(c)).astype(BF16)
    acc = jnp.dot(sc, w_ref[...].astype(BF16), preferred_element_type=F32)
    o_ref[...] = acc + b_ref[...]


def _adaln(c, w, b, tn=1024):
    bsz, d = c.shape
    n = w.shape[1]
    return pl.pallas_call(
        _adaln_kernel,
        grid=(n // tn,),
        in_specs=[
            pl.BlockSpec((bsz, d), lambda j: (0, 0)),
            pl.BlockSpec((d, tn), lambda j: (0, j)),
            pl.BlockSpec((1, tn), lambda j: (0, j)),
        ],
        out_specs=pl.BlockSpec((bsz, tn), lambda j: (0, j)),
        out_shape=jax.ShapeDtypeStruct((bsz, n), F32),
        compiler_params=_params("parallel"),
        name="adaln",
    )(c, w, b.reshape(1, n))


def _modulate_chunk(h_ref, x_ref, g_ref, shift_ref, scale_ref, chunk_rows):
    i, j = pl.program_id(0), pl.program_id(1)
    tm = x_ref.shape[0]
    start = pl.multiple_of(jnp.minimum(j * chunk_rows, tm - chunk_rows), BF16_ROWS)
    rows = pl.ds(start, chunk_rows)
    x = x_ref[rows, :]
    ms = jnp.mean(x * x, axis=-1, keepdims=True)
    y = x * lax.rsqrt(ms + EPS) * g_ref[...]
    h_ref[i % 2, rows, :] = (y * (1.0 + scale_ref[0]) + shift_ref[0]).astype(h_ref.dtype)


def _mod_mm_kernel(x_ref, g_ref, shift_ref, scale_ref, w_ref, o_ref, h_ref, *, chunk_rows):
    i = pl.program_id(0)

    @pl.when(i == 0)
    def _():
        _modulate_chunk(h_ref, x_ref, g_ref, shift_ref, scale_ref, chunk_rows)

    @pl.when(i > 0)
    def _():
        _modulate_chunk(h_ref, x_ref, g_ref, shift_ref, scale_ref, chunk_rows)
        o_ref[...] = jnp.dot(h_ref[(i - 1) % 2], w_ref[...], preferred_element_type=F32).astype(o_ref.dtype)


def _mod_mm_swiglu_kernel(x_ref, g_ref, shift_ref, scale_ref, wg_ref, wu_ref, o_ref, h_ref, *, chunk_rows):
    i = pl.program_id(0)

    @pl.when(i == 0)
    def _():
        _modulate_chunk(h_ref, x_ref, g_ref, shift_ref, scale_ref, chunk_rows)

    @pl.when(i > 0)
    def _():
        _modulate_chunk(h_ref, x_ref, g_ref, shift_ref, scale_ref, chunk_rows)
        h = h_ref[(i - 1) % 2]
        g = jnp.dot(h, wg_ref[...], preferred_element_type=F32)
        u = jnp.dot(h, wu_ref[...], preferred_element_type=F32)
        o_ref[...] = (g * jax.nn.sigmoid(g) * u).astype(o_ref.dtype)


def _mod_mm_call(kern, name, x, gain, mod3, shift_k, scale_k, seq, ws, w_col_offs, n, tm, tn):
    m, d = x.shape
    assert m % tm == 0 and n % tn == 0 and seq % tm == 0
    per_b = seq // tm
    n_i, n_j = m // tm, n // tn
    chunk_rows = -(-(-(-tm // n_j)) // BF16_ROWS) * BF16_ROWS
    assert chunk_rows <= tm

    def blk(i):
        return jnp.minimum(i, n_i - 1)

    def col(i, j):
        return jnp.where(i == 0, 0, j)

    w_specs = [pl.BlockSpec((d, tn), lambda i, j, off=off: (0, col(i, j) + off)) for off in w_col_offs]
    return pl.pallas_call(
        functools.partial(kern, chunk_rows=chunk_rows),
        grid=(n_i + 1, n_j),
        in_specs=[
            pl.BlockSpec((tm, d), lambda i, j: (blk(i), 0)),
            pl.BlockSpec((1, d), lambda i, j: (0, 0)),
            pl.BlockSpec((1, 1, d), lambda i, j: ((blk(i) // per_b) * 6 + shift_k, 0, 0)),
            pl.BlockSpec((1, 1, d), lambda i, j: ((blk(i) // per_b) * 6 + scale_k, 0, 0)),
        ] + w_specs,
        out_specs=pl.BlockSpec((tm, tn), lambda i, j: (jnp.maximum(i - 1, 0), col(i, j))),
        out_shape=jax.ShapeDtypeStruct((m, n), BF16),
        scratch_shapes=[pltpu.VMEM((2, tm, d), BF16)],
        compiler_params=_params("arbitrary", "arbitrary"),
        name=name,
    )(x, gain.reshape(1, d), mod3, mod3, *ws)


def _mod_matmul(x, gain, mod3, shift_k, scale_k, seq, w, tm, tn):
    return _mod_mm_call(_mod_mm_kernel, "mod_matmul", x, gain, mod3, shift_k, scale_k, seq, [w], [0],
                        w.shape[1], tm, tn)


def _mod_matmul_swiglu(x, gain, mod3, shift_k, scale_k, seq, w_gate_up, tm, tn):
    f = w_gate_up.shape[1] // 2
    return _mod_mm_call(_mod_mm_swiglu_kernel, "mod_matmul_swiglu", x, gain, mod3, shift_k, scale_k, seq,
                        [w_gate_up, w_gate_up], [0, f // tn], f, tm, tn)


def _rmsnorm_kernel(x_ref, g_ref, o_ref):
    x = x_ref[...].astype(F32)
    ms = jnp.mean(x * x, axis=-1, keepdims=True)
    o_ref[...] = (x * lax.rsqrt(ms + EPS) * g_ref[...]).astype(o_ref.dtype)


def _rmsnorm(x, gain, out_dtype, col_blk=0, tm=512):
    m = x.shape[0]
    d = gain.shape[0]
    return pl.pallas_call(
        _rmsnorm_kernel,
        grid=(m // tm,),
        in_specs=[
            pl.BlockSpec((tm, d), lambda i: (i, col_blk)),
            pl.BlockSpec((1, d), lambda i: (0, 0)),
        ],
        out_specs=pl.BlockSpec((tm, d), lambda i: (i, 0)),
        out_shape=jax.ShapeDtypeStruct((m, d), out_dtype),
        compiler_params=_params("parallel"),
        name="rmsnorm",
    )(x, gain.reshape(1, d))


def _mm_resid_kernel(a_ref, w_ref, x_ref, gate_ref, o_ref):
    acc = jnp.dot(a_ref[...], w_ref[...], preferred_element_type=F32)
    o_ref[...] = x_ref[...] + gate_ref[0] * acc


def _matmul_resid(a, w, x, mod3, gate_k, seq, tm, tn):
    m, kd = a.shape
    n = w.shape[1]
    assert m % tm == 0 and n % tn == 0 and seq % tm == 0
    per_b = seq // tm
    return pl.pallas_call(
        _mm_resid_kernel,
        grid=(m // tm, n // tn),
        in_specs=[
            pl.BlockSpec((tm, kd), lambda i, j: (i, 0)),
            pl.BlockSpec((kd, tn), lambda i, j: (0, j)),
            pl.BlockSpec((tm, tn), lambda i, j: (i, j)),
            pl.BlockSpec((1, 1, tn), lambda i, j: ((i // per_b) * 6 + gate_k, 0, j)),
        ],
        out_specs=pl.BlockSpec((tm, tn), lambda i, j: (i, j)),
        out_shape=jax.ShapeDtypeStruct((m, n), F32),
        compiler_params=_params("parallel", "parallel"),
        name="matmul_resid",
    )(a, w, x, mod3)


def _headmm_kernel(a_ref, w_ref, o_ref, *, scale):
    n_heads, din, dout = w_ref.shape
    for h in range(n_heads):
        acc = jnp.dot(a_ref[:, h * din:(h + 1) * din], w_ref[h], preferred_element_type=F32)
        o_ref[:, h * dout:(h + 1) * dout] = (acc * scale).astype(o_ref.dtype)


def _head_matmul(a, w, scale=1.0, tm=512):
    m = a.shape[0]
    h, din, dout = w.shape
    return pl.pallas_call(
        functools.partial(_headmm_kernel, scale=scale),
        grid=(m // tm,),
        in_specs=[
            pl.BlockSpec((tm, h * din), lambda i: (i, 0)),
            pl.BlockSpec((h, din, dout), lambda i: (0, 0, 0)),
        ],
        out_specs=pl.BlockSpec((tm, h * dout), lambda i: (i, 0)),
        out_shape=jax.ShapeDtypeStruct((m, h * dout), BF16),
        compiler_params=_params("parallel"),
        name="head_matmul",
    )(a, w)


def _nt_dot(a, b):
    return lax.dot_general(a, b, (((1,), (1,)), ((), ())), preferred_element_type=F32)


def _dilated_kernel(slope_ref, q_ref, k_ref, v_ref, o_ref, bias_ref, vaug_ref, *, seq, branches, tq):
    n_tiles = seq // tq

    @pl.when(pl.program_id(1) == 0)
    def _():
        slope = slope_ref[0][:, :1]
        qi = lax.broadcasted_iota(jnp.int32, (tq, seq), 0)
        kj = lax.broadcasted_iota(jnp.int32, (tq, seq), 1)
        d = qi + (seq - tq) - kj
        count = jnp.zeros((tq, seq), F32)
        for window, dil in branches:
            assert dil & (dil - 1) == 0
            count = count + jnp.where(((d & (dil - 1)) == 0) & (d <= window), 1.0, 0.0)
        bias = jnp.log2(jnp.maximum(count, 1.0)) - (slope * LOG2E) * d.astype(F32)
        bias_ref[...] = jnp.where((d >= 0) & (count > 0.0), bias, NEG_BIG)
        vaug_ref[:, HEAD_DIM:] = jnp.ones((seq, LANE), BF16)

    vaug_ref[:, :HEAD_DIM] = v_ref[0]

    for n in range(n_tiles):
        n_keys = (n + 1) * tq
        q = (q_ref[0, n * tq:(n + 1) * tq, :].astype(F32) * (HEAD_DIM ** -0.5 * LOG2E)).astype(BF16)
        s = _nt_dot(q, k_ref[0, :n_keys, :]) + bias_ref[:, seq - n_keys:]
        m = jnp.max(s, axis=-1, keepdims=True)
        p = jnp.exp2(s - m).astype(BF16)
        pv = jnp.dot(p, vaug_ref[:n_keys, :], preferred_element_type=F32)
        o_ref[0, n * tq:(n + 1) * tq, :] = (pv[:, :HEAD_DIM] / pv[:, HEAD_DIM:]).astype(o_ref.dtype)


def _dilated_attention(qkv, slopes3, bsz, seq, tq=256):
    h = N_HEADS
    assert seq % tq == 0
    kern = functools.partial(_dilated_kernel, seq=seq, branches=DILATED_BRANCHES, tq=tq)
    return pl.pallas_call(
        kern,
        grid=(h, bsz),
        in_specs=[
            pl.BlockSpec((1, 1, LANE), lambda j, b: (j, 0, 0)),
            pl.BlockSpec((1, seq, HEAD_DIM), lambda j, b: (b, 0, j)),
            pl.BlockSpec((1, seq, HEAD_DIM), lambda j, b: (b, 0, h + j)),
            pl.BlockSpec((1, seq, HEAD_DIM), lambda j, b: (b, 0, 2 * h + j)),
        ],
        out_specs=pl.BlockSpec((1, seq, HEAD_DIM), lambda j, b: (b, 0, j)),
        out_shape=jax.ShapeDtypeStruct((bsz, seq, h * HEAD_DIM), BF16),
        scratch_shapes=[
            pltpu.VMEM((tq, seq), F32),
            pltpu.VMEM((seq, HEAD_DIM + LANE), BF16),
        ],
        compiler_params=_params("parallel", "arbitrary"),
        name="dilated_attention",
    )(slopes3, qkv, qkv, qkv)


def _count_ge(keys, cand):
    return jnp.sum(jnp.where(keys >= cand, 1.0, 0.0), axis=-1, keepdims=True)


def _stack_heads(ref, col0, width, heads):
    parts = [ref[0, :, col0 + h * width:col0 + (h + 1) * width] for h in heads]
    return parts[0] if len(parts) == 1 else jnp.concatenate(parts, axis=0)


def _dsa_kernel(slope_ref, rows_ref, kidx_ref, qlat_ref, ckv_ref, o_ref, *, seq, topk, tq, n_groups, **static):
    tiles_per_group = seq // tq // n_groups
    group = pl.program_id(1) // tiles_per_group
    for g in range(n_groups):
        body = functools.partial(_dsa_body, slope_ref, rows_ref, kidx_ref, qlat_ref, ckv_ref, o_ref,
                                 seq=(g + 1) * tiles_per_group * tq, topk=topk, tq=tq, **static)
        pl.when(group == g)(body)


def _dsa_body(slope_ref, rows_ref, kidx_ref, qlat_ref, ckv_ref, o_ref, *, seq, topk, tq, qidx_col, widx_col,
              idx_heads_per_dot, att_heads_per_dot):
    t = pl.program_id(1)
    q_pos = t * tq + lax.broadcasted_iota(jnp.int32, (tq, seq), 0)
    k_pos = lax.broadcasted_iota(jnp.int32, (tq, seq), 1)
    causal = k_pos <= q_pos

    kidx = kidx_ref[0, :seq, :]
    w_all = rows_ref[0, :, widx_col:widx_col + LANE].astype(F32) * (IDX_DIM ** -0.5 * IDX_HEADS ** -0.5)
    score = jnp.zeros((tq, seq), F32)
    for h0 in range(0, IDX_HEADS, idx_heads_per_dot):
        heads = range(h0, h0 + idx_heads_per_dot)
        logits = _nt_dot(_stack_heads(rows_ref, qidx_col, IDX_DIM, heads), kidx)
        for j, h in enumerate(heads):
            score = score + w_all[:, h:h + 1] * jnp.maximum(logits[j * tq:(j + 1) * tq], 0.0)
    score = jnp.where(causal, score, NEG_BIG)
    score = jnp.where(score == 0.0, 0.0, score)

    bits = lax.bitcast_convert_type(score, jnp.int32)
    keys = jnp.where(bits < 0, bits ^ jnp.int32(0x7FFFFFFF), bits)

    kf = float(topk)
    int_min = jnp.int32(-(2 ** 31))
    keys_t = keys.T
    n_acc = 8

    def count_ge_t(cand):
        hit = jnp.where(keys_t >= cand, 1.0, 0.0).reshape(n_acc, seq // (8 * n_acc), 8, tq)
        partial = jnp.sum(jnp.sum(hit, axis=1), axis=0)
        return jnp.sum(partial, axis=0, keepdims=True)

    thr_t = jnp.where(count_ge_t(jnp.int32(0)) >= kf, jnp.int32(0), int_min)

    def bit_step(i, thr_t):
        cand = thr_t | (jnp.int32(1) << (30 - i))
        return jnp.where(count_ge_t(cand) >= kf, cand, thr_t)

    thr_t = lax.fori_loop(0, 31, bit_step, thr_t)
    thr = jnp.broadcast_to(thr_t, (tq, tq)).T[:, :1]

    gt = keys > thr
    eq = keys == thr
    n_gt = jnp.sum(jnp.where(gt, 1.0, 0.0), axis=-1, keepdims=True)
    n_eq = jnp.sum(jnp.where(eq, 1.0, 0.0), axis=-1, keepdims=True)
    need = kf - n_gt
    tie_rows = jnp.max(jnp.where(n_eq > need, 1.0, 0.0))

    def tie_cut():
        def cut_step(i, cut):
            cand = cut - (jnp.int32(1) << (n_bits - 1 - i))
            cnt = jnp.sum(jnp.where(eq & (k_pos <= cand), 1.0, 0.0), axis=-1, keepdims=True)
            return jnp.where(cnt >= need, cand, cut)

        n_bits = (seq - 1).bit_length()
        return lax.fori_loop(0, n_bits, cut_step, jnp.full((tq, 1), (1 << n_bits) - 1, jnp.int32))

    cut = lax.cond(tie_rows > 0.0, tie_cut, lambda: jnp.full((tq, 1), seq, jnp.int32))
    selected = (gt | (eq & (k_pos <= cut))) & causal
    mask_bias = jnp.where(selected, 0.0, NEG_BIG)

    k_pos_row = lax.broadcasted_iota(jnp.int32, (1, seq), 1).astype(F32)
    ckv = ckv_ref[0, :seq, :]
    for h0 in range(0, N_HEADS, att_heads_per_dot):
        heads = range(h0, h0 + att_heads_per_dot)
        s_all = _nt_dot(_stack_heads(qlat_ref, 0, KV_LATENT, heads), ckv)
        probs, sums = [], []
        for j, h in enumerate(heads):
            slope = slope_ref[h][:, :1]
            s = s_all[j * tq:(j + 1) * tq] + (slope * LOG2E) * k_pos_row + mask_bias
            p = jnp.exp2(s - jnp.max(s, axis=-1, keepdims=True))
            sums.append(jnp.sum(p, axis=-1, keepdims=True))
            probs.append(p.astype(BF16))
        p_all = probs[0] if len(probs) == 1 else jnp.concatenate(probs, axis=0)
        o_all = jnp.dot(p_all, ckv, preferred_element_type=F32)
        for j, h in enumerate(heads):
            o = o_all[j * tq:(j + 1) * tq] / sums[j]
            o_ref[0, :, h * KV_LATENT:(h + 1) * KV_LATENT] = o.astype(o_ref.dtype)


def _dsa_attention(proj, qlat, ckv, slopes3, bsz, seq, topk, tq=128, n_groups=8, idx_heads_per_dot=2,
                   att_heads_per_dot=2):
    h = N_HEADS
    width = proj.shape[-1]
    kern = functools.partial(_dsa_kernel, seq=seq, topk=topk, tq=tq, n_groups=n_groups,
                             qidx_col=_DSA_COLS["q_idx"], widx_col=_DSA_COLS["w_idx"],
                             idx_heads_per_dot=idx_heads_per_dot, att_heads_per_dot=att_heads_per_dot)
    return pl.pallas_call(
        kern,
        grid=(bsz, seq // tq),
        in_specs=[
            pl.BlockSpec((h, 1, LANE), lambda b, t: (0, 0, 0)),
            pl.BlockSpec((1, tq, width), lambda b, t: (b, t, 0)),
            pl.BlockSpec((1, seq, IDX_DIM), lambda b, t: (b, 0, _DSA_COLS["k_idx"] // IDX_DIM)),
            pl.BlockSpec((1, tq, h * KV_LATENT), lambda b, t: (b, t, 0)),
            pl.BlockSpec((1, seq, KV_LATENT), lambda b, t: (b, 0, 0)),
        ],
        out_specs=pl.BlockSpec((1, tq, h * KV_LATENT), lambda b, t: (b, t, 0)),
        out_shape=jax.ShapeDtypeStruct((bsz, seq, h * KV_LATENT), BF16),
        compiler_params=_params("parallel", "parallel"),
        name="dsa_attention",
    )(slopes3, proj, proj, qlat, ckv)


def _dsa_in_weight(w_in):
    return jnp.pad(w_in.astype(BF16), ((0, 0), (0, _DSA_WIDTH - w_in.shape[1])))


_TILE_IN = (1024, 1536)
_TILE_GATE_UP = (1024, 512)
_TILE_OUT = (512, 2048)
_TILE_DOWN = (1024, 512)


def _dilated_layer(proj, slopes3, bsz, seq):
    return _dilated_attention(proj.reshape(bsz, seq, -1), slopes3, bsz, seq).reshape(bsz * seq, -1)


def _dsa_layer(proj, kv_norm, w_uk, w_uv, slopes3, bsz, seq, topk):
    m = bsz * seq
    ckv = _rmsnorm(proj, kv_norm, BF16, col_blk=_DSA_COLS["c_kv"] // KV_LATENT)
    qlat = _head_matmul(proj, w_uk.astype(BF16), scale=HEAD_DIM ** -0.5 * LOG2E)
    o_lat = _dsa_attention(
        proj.reshape(bsz, seq, -1), qlat.reshape(bsz, seq, -1), ckv.reshape(bsz, seq, -1), slopes3, bsz, seq, topk
    )
    return _head_matmul(o_lat.reshape(m, -1), w_uv.astype(BF16))


def _forward(x, c, layers, final_norm):
    bsz, seq, d = x.shape
    m = bsz * seq
    topk = min(TOPK_MAX, seq // 4)
    slopes = jnp.exp2(-8.0 * jnp.arange(1, N_HEADS + 1, dtype=F32) / N_HEADS)
    slopes3 = jnp.broadcast_to(slopes[:, None, None], (N_HEADS, 1, LANE))
    xf = x.reshape(m, d)
    for i, (norm_attn, ada_w, ada_b, mixer_w, norm_ffn, w_gate_up, w_down) in enumerate(layers):
        mod3 = _adaln(c, ada_w, ada_b).reshape(bsz * 6, 1, d)
        if i % 2 == 0:
            w_in, w_out = mixer_w
            proj = _mod_matmul(xf, norm_attn, mod3, 0, 1, seq, w_in.astype(BF16), *_TILE_IN)
            o = _dilated_layer(proj, slopes3, bsz, seq)
        else:
            w_in, kv_norm, w_uk, w_uv, w_out = mixer_w
            proj = _mod_matmul(xf, norm_attn, mod3, 0, 1, seq, _dsa_in_weight(w_in), *_TILE_IN)
            o = _dsa_layer(proj, kv_norm, w_uk, w_uv, slopes3, bsz, seq, topk)
        xf = _matmul_resid(o, w_out.astype(BF16), xf, mod3, 2, seq, *_TILE_OUT)
        act = _mod_matmul_swiglu(xf, norm_ffn, mod3, 3, 4, seq, w_gate_up.astype(BF16), *_TILE_GATE_UP)
        xf = _matmul_resid(act, w_down.astype(BF16), xf, mod3, 5, seq, *_TILE_DOWN)
    return _rmsnorm(xf, final_norm, x.dtype).reshape(bsz, seq, d)


def kernel(x, c, l0_norm_attn, l0_ada_w, l0_ada_b, l0_w_in, l0_w_out, l0_norm_ffn, l0_w_gate_up, l0_w_down, l1_norm_attn, l1_ada_w, l1_ada_b, l1_w_in, l1_kv_norm, l1_w_uk, l1_w_uv, l1_w_out, l1_norm_ffn, l1_w_gate_up, l1_w_down, l2_norm_attn, l2_ada_w, l2_ada_b, l2_w_in, l2_w_out, l2_norm_ffn, l2_w_gate_up, l2_w_down, l3_norm_attn, l3_ada_w, l3_ada_b, l3_w_in, l3_kv_norm, l3_w_uk, l3_w_uv, l3_w_out, l3_norm_ffn, l3_w_gate_up, l3_w_down, final_norm):
    layers = (
        (l0_norm_attn, l0_ada_w, l0_ada_b, (l0_w_in, l0_w_out), l0_norm_ffn, l0_w_gate_up, l0_w_down),
        (l1_norm_attn, l1_ada_w, l1_ada_b, (l1_w_in, l1_kv_norm, l1_w_uk, l1_w_uv, l1_w_out),
         l1_norm_ffn, l1_w_gate_up, l1_w_down),
        (l2_norm_attn, l2_ada_w, l2_ada_b, (l2_w_in, l2_w_out), l2_norm_ffn, l2_w_gate_up, l2_w_down),
        (l3_norm_attn, l3_ada_w, l3_ada_b, (l3_w_in, l3_kv_norm, l3_w_uk, l3_w_uv, l3_w_out),
         l3_norm_ffn, l3_w_gate_up, l3_w_down),
    )
    return _forward(x, c, layers, final_norm)
```

```python
import functools

import jax
import jax.numpy as jnp
from jax import lax
from jax.experimental import pallas as pl
from jax.experimental.pallas import tpu as pltpu

N_HEADS = 16
HEAD_DIM = 128
DILATED_BRANCHES = ((128, 1), (512, 4), (2048, 16))
BLOCK = 128
KV_LATENT = 256
IDX_HEADS = 16
IDX_DIM = 128
TOPK_MAX = 256
EPS = 1e-6
NEG_BIG = -1e30
LOG2E = 1.4426950408889634

LANE = 128
BF16_ROWS = 16
VMEM_LIMIT = 56 * 1024 * 1024

F32 = jnp.float32
BF16 = jnp.bfloat16

_DSA_COLS = {
    "q": 0,
    "c_kv": N_HEADS * HEAD_DIM,
    "q_idx": N_HEADS * HEAD_DIM + KV_LATENT,
    "k_idx": N_HEADS * HEAD_DIM + KV_LATENT + IDX_HEADS * IDX_DIM,
    "w_idx": N_HEADS * HEAD_DIM + KV_LATENT + IDX_HEADS * IDX_DIM + IDX_DIM,
}
assert _DSA_COLS["c_kv"] % KV_LATENT == 0 and all(v % LANE == 0 for v in _DSA_COLS.values())
_DSA_WIDTH = -(-(_DSA_COLS["w_idx"] + IDX_HEADS) // 512) * 512


def _params(*sem):
    return pltpu.CompilerParams(dimension_semantics=sem, vmem_limit_bytes=VMEM_LIMIT)


def _adaln_kernel(c_ref, w_ref, b_ref, o_ref):
    c = c_ref[...]
    sc = (c * jax.nn.sigmoid(c)).astype(BF16)
    acc = jnp.dot(sc, w_ref[...].astype(BF16), preferred_element_type=F32)
    o_ref[...] = acc + b_ref[...]


def _adaln(c, w, b, tn=1024):
    bsz, d = c.shape
    n = w.shape[1]
    return pl.pallas_call(
        _adaln_kernel,
        grid=(n // tn,),
        in_specs=[
            pl.BlockSpec((bsz, d), lambda j: (0, 0)),
            pl.BlockSpec((d, tn), lambda j: (0, j)),
            pl.BlockSpec((1, tn), lambda j: (0, j)),
        ],
        out_specs=pl.BlockSpec((bsz, tn), lambda j: (0, j)),
        out_shape=jax.ShapeDtypeStruct((bsz, n), F32),
        compiler_params=_params("parallel"),
        name="adaln",
    )(c, w, b.reshape(1, n))


def _modulate_chunk(h_ref, x_ref, g_ref, shift_ref, scale_ref, chunk_rows):
    i, j = pl.program_id(0), pl.program_id(1)
    tm = x_ref.shape[0]
    start = pl.multiple_of(jnp.minimum(j * chunk_rows, tm - chunk_rows), BF16_ROWS)
    rows = pl.ds(start, chunk_rows)
    x = x_ref[rows, :]
    ms = jnp.mean(x * x, axis=-1, keepdims=True)
    col_scale = g_ref[...] * (1.0 + scale_ref[0])
    h_ref[i % 2, rows, :] = (x * lax.rsqrt(ms + EPS) * col_scale + shift_ref[0]).astype(h_ref.dtype)


def _mod_mm_kernel(x_ref, g_ref, shift_ref, scale_ref, w_ref, o_ref, h_ref, *, chunk_rows):
    i = pl.program_id(0)

    @pl.when(i == 0)
    def _():
        _modulate_chunk(h_ref, x_ref, g_ref, shift_ref, scale_ref, chunk_rows)

    @pl.when(i > 0)
    def _():
        _modulate_chunk(h_ref, x_ref, g_ref, shift_ref, scale_ref, chunk_rows)
        o_ref[...] = jnp.dot(h_ref[(i - 1) % 2], w_ref[...], preferred_element_type=F32).astype(o_ref.dtype)


def _mod_mm_swiglu_kernel(x_ref, g_ref, shift_ref, scale_ref, wg_ref, wu_ref, o_ref, h_ref, *, chunk_rows):
    i = pl.program_id(0)

    @pl.when(i == 0)
    def _():
        _modulate_chunk(h_ref, x_ref, g_ref, shift_ref, scale_ref, chunk_rows)

    @pl.when(i > 0)
    def _():
        _modulate_chunk(h_ref, x_ref, g_ref, shift_ref, scale_ref, chunk_rows)
        h = h_ref[(i - 1) % 2]
        g = jnp.dot(h, wg_ref[...], preferred_element_type=F32)
        u = jnp.dot(h, wu_ref[...], preferred_element_type=F32)
        o_ref[...] = (g * jax.nn.sigmoid(g) * u).astype(o_ref.dtype)


def _mod_mm_call(kern, name, x, gain, mod3, shift_k, scale_k, seq, ws, w_col_offs, n, tm, tn):
    m, d = x.shape
    assert m % tm == 0 and n % tn == 0 and seq % tm == 0
    per_b = seq // tm
    n_i, n_j = m // tm, n // tn
    chunk_rows = -(-(-(-tm // n_j)) // BF16_ROWS) * BF16_ROWS
    assert chunk_rows <= tm

    def blk(i):
        return jnp.minimum(i, n_i - 1)

    def col(i, j):
        return jnp.where(i == 0, 0, j)

    w_specs = [pl.BlockSpec((d, tn), lambda i, j, off=off: (0, col(i, j) + off)) for off in w_col_offs]
    return pl.pallas_call(
        functools.partial(kern, chunk_rows=chunk_rows),
        grid=(n_i + 1, n_j),
        in_specs=[
            pl.BlockSpec((tm, d), lambda i, j: (blk(i), 0)),
            pl.BlockSpec((1, d), lambda i, j: (0, 0)),
            pl.BlockSpec((1, 1, d), lambda i, j: ((blk(i) // per_b) * 6 + shift_k, 0, 0)),
            pl.BlockSpec((1, 1, d), lambda i, j: ((blk(i) // per_b) * 6 + scale_k, 0, 0)),
        ] + w_specs,
        out_specs=pl.BlockSpec((tm, tn), lambda i, j: (jnp.maximum(i - 1, 0), col(i, j))),
        out_shape=jax.ShapeDtypeStruct((m, n), BF16),
        scratch_shapes=[pltpu.VMEM((2, tm, d), BF16)],
        compiler_params=_params("arbitrary", "arbitrary"),
        name=name,
    )(x, gain.reshape(1, d), mod3, mod3, *ws)


def _mod_matmul(x, gain, mod3, shift_k, scale_k, seq, w, tm, tn):
    return _mod_mm_call(_mod_mm_kernel, "mod_matmul", x, gain, mod3, shift_k, scale_k, seq, [w], [0],
                        w.shape[1], tm, tn)


def _mod_matmul_swiglu(x, gain, mod3, shift_k, scale_k, seq, w_gate_up, tm, tn):
    f = w_gate_up.shape[1] // 2
    return _mod_mm_call(_mod_mm_swiglu_kernel, "mod_matmul_swiglu", x, gain, mod3, shift_k, scale_k, seq,
                        [w_gate_up, w_gate_up], [0, f // tn], f, tm, tn)


def _rmsnorm_kernel(x_ref, g_ref, o_ref):
    x = x_ref[...].astype(F32)
    ms = jnp.mean(x * x, axis=-1, keepdims=True)
    o_ref[...] = (x * lax.rsqrt(ms + EPS) * g_ref[...]).astype(o_ref.dtype)


def _rmsnorm(x, gain, out_dtype, col_blk=0, tm=512):
    m = x.shape[0]
    d = gain.shape[0]
    return pl.pallas_call(
        _rmsnorm_kernel,
        grid=(m // tm,),
        in_specs=[
            pl.BlockSpec((tm, d), lambda i: (i, col_blk)),
            pl.BlockSpec((1, d), lambda i: (0, 0)),
        ],
        out_specs=pl.BlockSpec((tm, d), lambda i: (i, 0)),
        out_shape=jax.ShapeDtypeStruct((m, d), out_dtype),
        compiler_params=_params("parallel"),
        name="rmsnorm",
    )(x, gain.reshape(1, d))


def _mm_resid_kernel(a_ref, w_ref, x_ref, gate_ref, o_ref):
    acc = jnp.dot(a_ref[...], w_ref[...], preferred_element_type=F32)
    o_ref[...] = x_ref[...] + gate_ref[0] * acc


def _matmul_resid(a, w, x, mod3, gate_k, seq, tm, tn):
    m, kd = a.shape
    n = w.shape[1]
    assert m % tm == 0 and n % tn == 0 and seq % tm == 0
    per_b = seq // tm
    return pl.pallas_call(
        _mm_resid_kernel,
        grid=(m // tm, n // tn),
        in_specs=[
            pl.BlockSpec((tm, kd), lambda i, j: (i, 0)),
            pl.BlockSpec((kd, tn), lambda i, j: (0, j)),
            pl.BlockSpec((tm, tn), lambda i, j: (i, j)),
            pl.BlockSpec((1, 1, tn), lambda i, j: ((i // per_b) * 6 + gate_k, 0, j)),
        ],
        out_specs=pl.BlockSpec((tm, tn), lambda i, j: (i, j)),
        out_shape=jax.ShapeDtypeStruct((m, n), F32),
        compiler_params=_params("parallel", "parallel"),
        name="matmul_resid",
    )(a, w, x, mod3)


def _headmm_kernel(a_ref, w_ref, o_ref, *, scale):
    n_heads, din, dout = w_ref.shape
    for h in range(n_heads):
        acc = jnp.dot(a_ref[:, h * din:(h + 1) * din], w_ref[h], preferred_element_type=F32)
        o_ref[:, h * dout:(h + 1) * dout] = (acc * scale).astype(o_ref.dtype)


def _head_matmul(a, w, scale=1.0, tm=512):
    m = a.shape[0]
    h, din, dout = w.shape
    return pl.pallas_call(
        functools.partial(_headmm_kernel, scale=scale),
        grid=(m // tm,),
        in_specs=[
            pl.BlockSpec((tm, h * din), lambda i: (i, 0)),
            pl.BlockSpec((h, din, dout), lambda i: (0, 0, 0)),
        ],
        out_specs=pl.BlockSpec((tm, h * dout), lambda i: (i, 0)),
        out_shape=jax.ShapeDtypeStruct((m, h * dout), BF16),
        compiler_params=_params("parallel"),
        name="head_matmul",
    )(a, w)


def _nt_dot(a, b):
    return lax.dot_general(a, b, (((1,), (1,)), ((), ())), preferred_element_type=F32)


def _dilated_kernel(slope_ref, q_ref, k_ref, v_ref, o_ref, bias_ref, vaug_ref, *, seq, branches, tq):
    n_tiles = seq // tq

    @pl.when(pl.program_id(1) == 0)
    def _():
        slope = slope_ref[0][:, :1]
        qi = lax.broadcasted_iota(jnp.int32, (tq, seq), 0)
        kj = lax.broadcasted_iota(jnp.int32, (tq, seq), 1)
        d = qi + (seq - tq) - kj
        count = jnp.zeros((tq, seq), F32)
        for window, dil in branches:
            assert dil & (dil - 1) == 0
            count = count + jnp.where(((d & (dil - 1)) == 0) & (d <= window), 1.0, 0.0)
        bias = jnp.log2(jnp.maximum(count, 1.0)) - (slope * LOG2E) * d.astype(F32)
        bias_ref[...] = jnp.where((d >= 0) & (count > 0.0), bias, NEG_BIG)
        vaug_ref[:, HEAD_DIM:] = jnp.ones((seq, LANE), BF16)

    vaug_ref[:, :HEAD_DIM] = v_ref[0]

    for n in range(n_tiles):
        n_keys = (n + 1) * tq
        q = (q_ref[0, n * tq:(n + 1) * tq, :].astype(F32) * (HEAD_DIM ** -0.5 * LOG2E)).astype(BF16)
        s = _nt_dot(q, k_ref[0, :n_keys, :]) + bias_ref[:, seq - n_keys:]
        m = jnp.max(s, axis=-1, keepdims=True)
        p = jnp.exp2(s - m).astype(BF16)
        pv = jnp.dot(p, vaug_ref[:n_keys, :], preferred_element_type=F32)
        o_ref[0, n * tq:(n + 1) * tq, :] = (pv[:, :HEAD_DIM] / pv[:, HEAD_DIM:]).astype(o_ref.dtype)


def _dilated_attention(qkv, slopes3, bsz, seq, tq=256):
    h = N_HEADS
    assert seq % tq == 0
    kern = functools.partial(_dilated_kernel, seq=seq, branches=DILATED_BRANCHES, tq=tq)
    return pl.pallas_call(
        kern,
        grid=(h, bsz),
        in_specs=[
            pl.BlockSpec((1, 1, LANE), lambda j, b: (j, 0, 0)),
            pl.BlockSpec((1, seq, HEAD_DIM), lambda j, b: (b, 0, j)),
            pl.BlockSpec((1, seq, HEAD_DIM), lambda j, b: (b, 0, h + j)),
            pl.BlockSpec((1, seq, HEAD_DIM), lambda j, b: (b, 0, 2 * h + j)),
        ],
        out_specs=pl.BlockSpec((1, seq, HEAD_DIM), lambda j, b: (b, 0, j)),
        out_shape=jax.ShapeDtypeStruct((bsz, seq, h * HEAD_DIM), BF16),
        scratch_shapes=[
            pltpu.VMEM((tq, seq), F32),
            pltpu.VMEM((seq, HEAD_DIM + LANE), BF16),
        ],
        compiler_params=_params("parallel", "arbitrary"),
        name="dilated_attention",
    )(slopes3, qkv, qkv, qkv)


def _count_ge(keys, cand):
    return jnp.sum(jnp.where(keys >= cand, 1.0, 0.0), axis=-1, keepdims=True)


def _stack_heads(ref, col0, width, heads):
    parts = [ref[0, :, col0 + h * width:col0 + (h + 1) * width] for h in heads]
    return parts[0] if len(parts) == 1 else jnp.concatenate(parts, axis=0)


def _dsa_kernel(slope_ref, rows_ref, kidx_ref, qlat_ref, ckv_ref, o_ref, *, seq, topk, tq, n_groups, **static):
    tiles_per_group = seq // tq // n_groups
    group = pl.program_id(0) // tiles_per_group
    for g in range(n_groups):
        body = functools.partial(_dsa_body, slope_ref, rows_ref, kidx_ref, qlat_ref, ckv_ref, o_ref,
                                 seq=(g + 1) * tiles_per_group * tq, topk=topk, tq=tq, **static)
        pl.when(group == g)(body)


def _dsa_body(slope_ref, rows_ref, kidx_ref, qlat_ref, ckv_ref, o_ref, *, seq, topk, tq, qidx_col, widx_col,
              idx_heads_per_dot, att_heads_per_dot):
    t = pl.program_id(0)
    q_pos = t * tq + lax.broadcasted_iota(jnp.int32, (tq, seq), 0)
    k_pos = lax.broadcasted_iota(jnp.int32, (tq, seq), 1)
    causal = k_pos <= q_pos

    kidx = kidx_ref[0, :seq, :]
    w_all = rows_ref[0, :, widx_col:widx_col + LANE].astype(F32) * (IDX_DIM ** -0.5 * IDX_HEADS ** -0.5)
    score = jnp.zeros((tq, seq), F32)
    for h0 in range(0, IDX_HEADS, idx_heads_per_dot):
        heads = range(h0, h0 + idx_heads_per_dot)
        logits = _nt_dot(_stack_heads(rows_ref, qidx_col, IDX_DIM, heads), kidx)
        for j, h in enumerate(heads):
            score = score + w_all[:, h:h + 1] * jnp.maximum(logits[j * tq:(j + 1) * tq], 0.0)
    score = jnp.where(causal, score, NEG_BIG)
    score = jnp.where(score == 0.0, 0.0, score)

    bits = lax.bitcast_convert_type(score, jnp.int32)
    keys = jnp.where(bits < 0, bits ^ jnp.int32(0x7FFFFFFF), bits)

    kf = float(topk)
    int_min = jnp.int32(-(2 ** 31))
    keys_t = keys.T
    n_acc = 8

    def count_ge_t(cand):
        hit = jnp.where(keys_t >= cand, 1.0, 0.0).reshape(n_acc, seq // (8 * n_acc), 8, tq)
        partial = jnp.sum(jnp.sum(hit, axis=1), axis=0)
        return jnp.sum(partial, axis=0, keepdims=True)

    thr_t = jnp.where(count_ge_t(jnp.int32(0)) >= kf, jnp.int32(0), int_min)

    def bit_step(i, thr_t):
        cand = thr_t | (jnp.int32(1) << (30 - i))
        return jnp.where(count_ge_t(cand) >= kf, cand, thr_t)

    thr_t = lax.fori_loop(0, 31, bit_step, thr_t)
    thr = jnp.broadcast_to(thr_t, (tq, tq)).T[:, :1]

    gt = keys > thr
    eq = keys == thr
    n_gt = jnp.sum(jnp.where(gt, 1.0, 0.0), axis=-1, keepdims=True)
    n_eq = jnp.sum(jnp.where(eq, 1.0, 0.0), axis=-1, keepdims=True)
    need = kf - n_gt
    tie_rows = jnp.max(jnp.where(n_eq > need, 1.0, 0.0))

    def tie_cut():
        def cut_step(i, cut):
            cand = cut - (jnp.int32(1) << (n_bits - 1 - i))
            cnt = jnp.sum(jnp.where(eq & (k_pos <= cand), 1.0, 0.0), axis=-1, keepdims=True)
            return jnp.where(cnt >= need, cand, cut)

        n_bits = (seq - 1).bit_length()
        return lax.fori_loop(0, n_bits, cut_step, jnp.full((tq, 1), (1 << n_bits) - 1, jnp.int32))

    cut = lax.cond(tie_rows > 0.0, tie_cut, lambda: jnp.full((tq, 1), seq, jnp.int32))
    selected = (gt | (eq & (k_pos <= cut))) & causal
    mask_bias = jnp.where(selected, 0.0, NEG_BIG)

    k_pos_row = lax.broadcasted_iota(jnp.int32, (1, seq), 1).astype(F32)
    ckv = ckv_ref[0, :seq, :]
    for h0 in range(0, N_HEADS, att_heads_per_dot):
        heads = range(h0, h0 + att_heads_per_dot)
        s_all = _nt_dot(_stack_heads(qlat_ref, 0, KV_LATENT, heads), ckv)
        probs, sums = [], []
        for j, h in enumerate(heads):
            slope = slope_ref[h][:, :1]
            s = s_all[j * tq:(j + 1) * tq] + (slope * LOG2E) * k_pos_row + mask_bias
            p = jnp.exp2(s - jnp.max(s, axis=-1, keepdims=True))
            sums.append(jnp.sum(p, axis=-1, keepdims=True))
            probs.append(p.astype(BF16))
        p_all = probs[0] if len(probs) == 1 else jnp.concatenate(probs, axis=0)
        o_all = jnp.dot(p_all, ckv, preferred_element_type=F32)
        for j, h in enumerate(heads):
            o = o_all[j * tq:(j + 1) * tq] / sums[j]
            o_ref[0, :, h * KV_LATENT:(h + 1) * KV_LATENT] = o.astype(o_ref.dtype)


def _dsa_attention(proj, qlat, ckv, slopes3, bsz, seq, topk, tq=128, n_groups=8, idx_heads_per_dot=2,
                   att_heads_per_dot=2):
    h = N_HEADS
    width = proj.shape[-1]
    kern = functools.partial(_dsa_kernel, seq=seq, topk=topk, tq=tq, n_groups=n_groups,
                             qidx_col=_DSA_COLS["q_idx"], widx_col=_DSA_COLS["w_idx"],
                             idx_heads_per_dot=idx_heads_per_dot, att_heads_per_dot=att_heads_per_dot)
    return pl.pallas_call(
        kern,
        grid=(seq // tq, bsz),
        in_specs=[
            pl.BlockSpec((h, 1, LANE), lambda t, b: (0, 0, 0)),
            pl.BlockSpec((1, tq, width), lambda t, b: (b, t, 0)),
            pl.BlockSpec((1, seq, IDX_DIM), lambda t, b: (b, 0, _DSA_COLS["k_idx"] // IDX_DIM)),
            pl.BlockSpec((1, tq, h * KV_LATENT), lambda t, b: (b, t, 0)),
            pl.BlockSpec((1, seq, KV_LATENT), lambda t, b: (b, 0, 0)),
        ],
        out_specs=pl.BlockSpec((1, tq, h * KV_LATENT), lambda t, b: (b, t, 0)),
        out_shape=jax.ShapeDtypeStruct((bsz, seq, h * KV_LATENT), BF16),
        compiler_params=_params("parallel", "parallel"),
        name="dsa_attention",
    )(slopes3, proj, proj, qlat, ckv)


def _dsa_in_weight(w_in):
    return jnp.pad(w_in.astype(BF16), ((0, 0), (0, _DSA_WIDTH - w_in.shape[1])))


_TILE_IN = (1024, 1536)
_TILE_GATE_UP = (1024, 512)
_TILE_OUT = (512, 2048)
_TILE_DOWN = (1024, 512)


def _dilated_layer(proj, slopes3, bsz, seq):
    return _dilated_attention(proj.reshape(bsz, seq, -1), slopes3, bsz, seq).reshape(bsz * seq, -1)


def _dsa_layer(proj, kv_norm, w_uk, w_uv, slopes3, bsz, seq, topk):
    m = bsz * seq
    ckv = _rmsnorm(proj, kv_norm, BF16, col_blk=_DSA_COLS["c_kv"] // KV_LATENT)
    qlat = _head_matmul(proj, w_uk.astype(BF16), scale=HEAD_DIM ** -0.5 * LOG2E)
    o_lat = _dsa_attention(
        proj.reshape(bsz, seq, -1), qlat.reshape(bsz, seq, -1), ckv.reshape(bsz, seq, -1), slopes3, bsz, seq, topk
    )
    return _head_matmul(o_lat.reshape(m, -1), w_uv.astype(BF16))


def _forward(x, c, layers, final_norm):
    bsz, seq, d = x.shape
    m = bsz * seq
    topk = min(TOPK_MAX, seq // 4)
    slopes = jnp.exp2(-8.0 * jnp.arange(1, N_HEADS + 1, dtype=F32) / N_HEADS)
    slopes3 = jnp.broadcast_to(slopes[:, None, None], (N_HEADS, 1, LANE))
    xf = x.reshape(m, d)
    for i, (norm_attn, ada_w, ada_b, mixer_w, norm_ffn, w_gate_up, w_down) in enumerate(layers):
        mod3 = _adaln(c, ada_w, ada_b).reshape(bsz * 6, 1, d)
        if i % 2 == 0:
            w_in, w_out = mixer_w
            proj = _mod_matmul(xf, norm_attn, mod3, 0, 1, seq, w_in.astype(BF16), *_TILE_IN)
            o = _dilated_layer(proj, slopes3, bsz, seq)
        else:
            w_in, kv_norm, w_uk, w_uv, w_out = mixer_w
            proj = _mod_matmul(xf, norm_attn, mod3, 0, 1, seq, _dsa_in_weight(w_in), *_TILE_IN)
            o = _dsa_layer(proj, kv_norm, w_uk, w_uv, slopes3, bsz, seq, topk)
        xf = _matmul_resid(o, w_out.astype(BF16), xf, mod3, 2, seq, *_TILE_OUT)
        act = _mod_matmul_swiglu(xf, norm_ffn, mod3, 3, 4, seq, w_gate_up.astype(BF16), *_TILE_GATE_UP)
        xf = _matmul_resid(act, w_down.astype(BF16), xf, mod3, 5, seq, *_TILE_DOWN)
    return _rmsnorm(xf, final_norm, x.dtype).reshape(bsz, seq, d)


def kernel(x, c, l0_norm_attn, l0_ada_w, l0_ada_b, l0_w_in, l0_w_out, l0_norm_ffn, l0_w_gate_up, l0_w_down, l1_norm_attn, l1_ada_w, l1_ada_b, l1_w_in, l1_kv_norm, l1_w_uk, l1_w_uv, l1_w_out, l1_norm_ffn, l1_w_gate_up, l1_w_down, l2_norm_attn, l2_ada_w, l2_ada_b, l2_w_in, l2_w_out, l2_norm_ffn, l2_w_gate_up, l2_w_down, l3_norm_attn, l3_ada_w, l3_ada_b, l3_w_in, l3_kv_norm, l3_w_uk, l3_w_uv, l3_w_out, l3_norm_ffn, l3_w_gate_up, l3_w_down, final_norm):
    layers = (
        (l0_norm_attn, l0_ada_w, l0_ada_b, (l0_w_in, l0_w_out), l0_norm_ffn, l0_w_gate_up, l0_w_down),
        (l1_norm_attn, l1_ada_w, l1_ada_b, (l1_w_in, l1_kv_norm, l1_w_uk, l1_w_uv, l1_w_out),
         l1_norm_ffn, l1_w_gate_up, l1_w_down),
        (l2_norm_attn, l2_ada_w, l2_ada_b, (l2_w_in, l2_w_out), l2_norm_ffn, l2_w_gate_up, l2_w_down),
        (l3_norm_attn, l3_ada_w, l3_ada_b, (l3_w_in, l3_kv_norm, l3_w_uk, l3_w_uv, l3_w_out),
         l3_norm_ffn, l3_w_gate_up, l3_w_down),
    )
    return _forward(x, c, layers, final_norm)
```

```python
import functools

import jax
import jax.numpy as jnp
from jax import lax
from jax.experimental import pallas as pl
from jax.experimental.pallas import tpu as pltpu

N_HEADS = 16
HEAD_DIM = 128
DILATED_BRANCHES = ((128, 1), (512, 4), (2048, 16))
KV_LATENT = 256
IDX_HEADS = 16
IDX_DIM = 128
TOPK_MAX = 256
EPS = 1e-6
NEG_BIG = -1e30
LOG2E = 1.4426950408889634

LANE = 128
BF16_ROWS = 16
VMEM_LIMIT = 56 * 1024 * 1024

F32 = jnp.float32
BF16 = jnp.bfloat16

_TILE_IN = (1024, 1536)
_TILE_GATE_UP = (1024, 512)
_TILE_OUT = (512, 2048)
_TILE_DOWN = (1024, 512)
_TILE_ROWS = 512
_DILATED_TQ = 256
_DSA_TQ = 128
_DSA_KEY_GROUPS = 4
_DSA_HEADS_PER_DOT = 2

_DSA_COLS = {
    "q": 0,
    "c_kv": N_HEADS * HEAD_DIM,
    "q_idx": N_HEADS * HEAD_DIM + KV_LATENT,
    "k_idx": N_HEADS * HEAD_DIM + KV_LATENT + IDX_HEADS * IDX_DIM,
    "w_idx": N_HEADS * HEAD_DIM + KV_LATENT + IDX_HEADS * IDX_DIM + IDX_DIM,
}
assert _DSA_COLS["c_kv"] % KV_LATENT == 0 and all(v % LANE == 0 for v in _DSA_COLS.values())
_DSA_WIDTH = -(-(_DSA_COLS["w_idx"] + IDX_HEADS) // _TILE_IN[1]) * _TILE_IN[1]


def _params(*sem):
    return pltpu.CompilerParams(dimension_semantics=sem, vmem_limit_bytes=VMEM_LIMIT)


def _adaln_kernel(c_ref, w_ref, b_ref, o_ref):
    c = c_ref[...]
    sc = (c * jax.nn.sigmoid(c)).astype(BF16)
    acc = jnp.dot(sc, w_ref[...].astype(BF16), preferred_element_type=F32)
    o_ref[...] = acc + b_ref[...]


def _adaln(c, w, b, tn=1024):
    bsz, d = c.shape
    n = w.shape[1]
    return pl.pallas_call(
        _adaln_kernel,
        grid=(n // tn,),
        in_specs=[
            pl.BlockSpec((bsz, d), lambda j: (0, 0)),
            pl.BlockSpec((d, tn), lambda j: (0, j)),
            pl.BlockSpec((1, tn), lambda j: (0, j)),
        ],
        out_specs=pl.BlockSpec((bsz, tn), lambda j: (0, j)),
        out_shape=jax.ShapeDtypeStruct((bsz, n), F32),
        compiler_params=_params("parallel"),
        name="adaln",
    )(c, w, b.reshape(1, n))


def _modulate_chunk(h_ref, x_ref, g_ref, shift_ref, scale_ref, chunk_rows):
    i, j = pl.program_id(0), pl.program_id(1)
    tm = x_ref.shape[0]
    start = pl.multiple_of(jnp.minimum(j * chunk_rows, tm - chunk_rows), BF16_ROWS)
    rows = pl.ds(start, chunk_rows)
    x = x_ref[rows, :]
    ms = jnp.mean(x * x, axis=-1, keepdims=True)
    col_scale = g_ref[...] * (1.0 + scale_ref[0])
    h_ref[i % 2, rows, :] = (x * lax.rsqrt(ms + EPS) * col_scale + shift_ref[0]).astype(h_ref.dtype)


def _mod_mm_kernel(x_ref, g_ref, shift_ref, scale_ref, w_ref, o_ref, h_ref, *, chunk_rows):
    i = pl.program_id(0)

    @pl.when(i == 0)
    def _():
        _modulate_chunk(h_ref, x_ref, g_ref, shift_ref, scale_ref, chunk_rows)

    @pl.when(i > 0)
    def _():
        _modulate_chunk(h_ref, x_ref, g_ref, shift_ref, scale_ref, chunk_rows)
        o_ref[...] = jnp.dot(h_ref[(i - 1) % 2], w_ref[...], preferred_element_type=F32).astype(o_ref.dtype)


def _mod_mm_swiglu_kernel(x_ref, g_ref, shift_ref, scale_ref, wg_ref, wu_ref, o_ref, h_ref, *, chunk_rows):
    i = pl.program_id(0)

    @pl.when(i == 0)
    def _():
        _modulate_chunk(h_ref, x_ref, g_ref, shift_ref, scale_ref, chunk_rows)

    @pl.when(i > 0)
    def _():
        _modulate_chunk(h_ref, x_ref, g_ref, shift_ref, scale_ref, chunk_rows)
        h = h_ref[(i - 1) % 2]
        g = jnp.dot(h, wg_ref[...], preferred_element_type=F32)
        u = jnp.dot(h, wu_ref[...], preferred_element_type=F32)
        o_ref[...] = (g * jax.nn.sigmoid(g) * u).astype(o_ref.dtype)


def _mod_mm_call(kern, name, x, gain, mod3, shift_k, scale_k, seq, ws, w_col_offs, n, tm, tn):
    m, d = x.shape
    assert m % tm == 0 and n % tn == 0 and seq % tm == 0
    per_b = seq // tm
    n_i, n_j = m // tm, n // tn
    chunk_rows = -(-(-(-tm // n_j)) // BF16_ROWS) * BF16_ROWS
    assert chunk_rows <= tm

    def blk(i):
        return jnp.minimum(i, n_i - 1)

    def col(i, j):
        return jnp.where(i == 0, 0, j)

    w_specs = [pl.BlockSpec((d, tn), lambda i, j, off=off: (0, col(i, j) + off)) for off in w_col_offs]
    return pl.pallas_call(
        functools.partial(kern, chunk_rows=chunk_rows),
        grid=(n_i + 1, n_j),
        in_specs=[
            pl.BlockSpec((tm, d), lambda i, j: (blk(i), 0)),
            pl.BlockSpec((1, d), lambda i, j: (0, 0)),
            pl.BlockSpec((1, 1, d), lambda i, j: ((blk(i) // per_b) * 6 + shift_k, 0, 0)),
            pl.BlockSpec((1, 1, d), lambda i, j: ((blk(i) // per_b) * 6 + scale_k, 0, 0)),
        ] + w_specs,
        out_specs=pl.BlockSpec((tm, tn), lambda i, j: (jnp.maximum(i - 1, 0), col(i, j))),
        out_shape=jax.ShapeDtypeStruct((m, n), BF16),
        scratch_shapes=[pltpu.VMEM((2, tm, d), BF16)],
        compiler_params=_params("arbitrary", "arbitrary"),
        name=name,
    )(x, gain.reshape(1, d), mod3, mod3, *ws)


def _mod_matmul(x, gain, mod3, shift_k, scale_k, seq, w, tm, tn):
    return _mod_mm_call(_mod_mm_kernel, "mod_matmul", x, gain, mod3, shift_k, scale_k, seq, [w], [0],
                        w.shape[1], tm, tn)


def _mod_matmul_swiglu(x, gain, mod3, shift_k, scale_k, seq, w_gate_up, tm, tn):
    f = w_gate_up.shape[1] // 2
    return _mod_mm_call(_mod_mm_swiglu_kernel, "mod_matmul_swiglu", x, gain, mod3, shift_k, scale_k, seq,
                        [w_gate_up, w_gate_up], [0, f // tn], f, tm, tn)


def _rmsnorm_kernel(x_ref, g_ref, o_ref):
    x = x_ref[...].astype(F32)
    ms = jnp.mean(x * x, axis=-1, keepdims=True)
    o_ref[...] = (x * lax.rsqrt(ms + EPS) * g_ref[...]).astype(o_ref.dtype)


def _rmsnorm(x, gain, out_dtype, tm=_TILE_ROWS):
    m, d = x.shape
    return pl.pallas_call(
        _rmsnorm_kernel,
        grid=(m // tm,),
        in_specs=[
            pl.BlockSpec((tm, d), lambda i: (i, 0)),
            pl.BlockSpec((1, d), lambda i: (0, 0)),
        ],
        out_specs=pl.BlockSpec((tm, d), lambda i: (i, 0)),
        out_shape=jax.ShapeDtypeStruct((m, d), out_dtype),
        compiler_params=_params("parallel"),
        name="rmsnorm",
    )(x, gain.reshape(1, d))


def _mm_resid_kernel(a_ref, w_ref, x_ref, gate_ref, o_ref):
    acc = jnp.dot(a_ref[...], w_ref[...], preferred_element_type=F32)
    o_ref[...] = x_ref[...] + gate_ref[0] * acc


def _matmul_resid(a, w, x, mod3, gate_k, seq, tm, tn):
    m, kd = a.shape
    n = w.shape[1]
    assert m % tm == 0 and n % tn == 0 and seq % tm == 0
    per_b = seq // tm
    return pl.pallas_call(
        _mm_resid_kernel,
        grid=(m // tm, n // tn),
        in_specs=[
            pl.BlockSpec((tm, kd), lambda i, j: (i, 0)),
            pl.BlockSpec((kd, tn), lambda i, j: (0, j)),
            pl.BlockSpec((tm, tn), lambda i, j: (i, j)),
            pl.BlockSpec((1, 1, tn), lambda i, j: ((i // per_b) * 6 + gate_k, 0, j)),
        ],
        out_specs=pl.BlockSpec((tm, tn), lambda i, j: (i, j)),
        out_shape=jax.ShapeDtypeStruct((m, n), F32),
        compiler_params=_params("parallel", "parallel"),
        name="matmul_resid",
    )(a, w, x, mod3)


def _head_project(a_ref, w_ref, scale=1.0):
    n_heads, din, dout = w_ref.shape
    parts = []
    for h in range(n_heads):
        acc = jnp.dot(a_ref[:, h * din:(h + 1) * din], w_ref[h], preferred_element_type=F32)
        parts.append((acc * scale).astype(BF16) if scale != 1.0 else acc.astype(BF16))
    return jnp.concatenate(parts, axis=1)


def _heads_mm_resid_kernel(a_ref, wh_ref, w_ref, x_ref, gate_ref, o_ref):
    acc = jnp.dot(_head_project(a_ref, wh_ref), w_ref[...], preferred_element_type=F32)
    o_ref[...] = x_ref[...] + gate_ref[0] * acc


def _heads_matmul_resid(a, w_heads, w, x, mod3, gate_k, seq, tm):
    m = a.shape[0]
    kd, n = w.shape
    assert m % tm == 0 and seq % tm == 0 and w_heads.shape[0] * w_heads.shape[2] == kd
    per_b = seq // tm
    return pl.pallas_call(
        _heads_mm_resid_kernel,
        grid=(m // tm,),
        in_specs=[
            pl.BlockSpec((tm, a.shape[1]), lambda i: (i, 0)),
            pl.BlockSpec(w_heads.shape, lambda i: (0, 0, 0)),
            pl.BlockSpec((kd, n), lambda i: (0, 0)),
            pl.BlockSpec((tm, n), lambda i: (i, 0)),
            pl.BlockSpec((1, 1, n), lambda i: ((i // per_b) * 6 + gate_k, 0, 0)),
        ],
        out_specs=pl.BlockSpec((tm, n), lambda i: (i, 0)),
        out_shape=jax.ShapeDtypeStruct((m, n), F32),
        compiler_params=_params("parallel"),
        name="heads_matmul_resid",
    )(a, w_heads, w, x, mod3)


def _qlat_ckv_kernel(q_ref, ckv_ref, w_ref, g_ref, qlat_ref, ckvn_ref, *, scale):
    qlat_ref[...] = _head_project(q_ref, w_ref, scale)
    c = ckv_ref[...].astype(F32)
    ms = jnp.mean(c * c, axis=-1, keepdims=True)
    ckvn_ref[...] = (c * lax.rsqrt(ms + EPS) * g_ref[...]).astype(ckvn_ref.dtype)


def _qlat_and_ckv(proj, w_uk, kv_norm, scale, tm=2 * _TILE_ROWS):
    m = proj.shape[0]
    h, din, dout = w_uk.shape
    c = kv_norm.shape[0]
    return pl.pallas_call(
        functools.partial(_qlat_ckv_kernel, scale=scale),
        grid=(m // tm,),
        in_specs=[
            pl.BlockSpec((tm, h * din), lambda i: (i, _DSA_COLS["q"] // (h * din))),
            pl.BlockSpec((tm, c), lambda i: (i, _DSA_COLS["c_kv"] // c)),
            pl.BlockSpec((h, din, dout), lambda i: (0, 0, 0)),
            pl.BlockSpec((1, c), lambda i: (0, 0)),
        ],
        out_specs=[
            pl.BlockSpec((tm, h * dout), lambda i: (i, 0)),
            pl.BlockSpec((tm, c), lambda i: (i, 0)),
        ],
        out_shape=[jax.ShapeDtypeStruct((m, h * dout), BF16), jax.ShapeDtypeStruct((m, c), BF16)],
        compiler_params=_params("parallel"),
        name="qlat_ckv",
    )(proj, proj, w_uk, kv_norm.reshape(1, c))


def _nt_dot(a, b):
    return lax.dot_general(a, b, (((1,), (1,)), ((), ())), preferred_element_type=F32)


def _dilated_kernel(slope_ref, q_ref, k_ref, v_ref, o_ref, bias_ref, vaug_ref, *, seq, branches, tq):
    n_tiles = seq // tq

    @pl.when(pl.program_id(1) == 0)
    def _():
        slope = slope_ref[0][:, :1]
        qi = lax.broadcasted_iota(jnp.int32, (tq, seq), 0)
        kj = lax.broadcasted_iota(jnp.int32, (tq, seq), 1)
        d = qi + (seq - tq) - kj
        count = jnp.zeros((tq, seq), F32)
        for window, dil in branches:
            assert dil & (dil - 1) == 0
            count = count + jnp.where(((d & (dil - 1)) == 0) & (d <= window), 1.0, 0.0)
        bias = jnp.log2(jnp.maximum(count, 1.0)) - (slope * LOG2E) * d.astype(F32)
        bias_ref[...] = jnp.where((d >= 0) & (count > 0.0), bias, NEG_BIG)
        vaug_ref[:, HEAD_DIM:] = jnp.ones((seq, LANE), BF16)

    vaug_ref[:, :HEAD_DIM] = v_ref[0]

    for n in range(n_tiles):
        n_keys = (n + 1) * tq
        q = (q_ref[0, n * tq:(n + 1) * tq, :].astype(F32) * (HEAD_DIM ** -0.5 * LOG2E)).astype(BF16)
        s = _nt_dot(q, k_ref[0, :n_keys, :]) + bias_ref[:, seq - n_keys:]
        m = jnp.max(s, axis=-1, keepdims=True)
        p = jnp.exp2(s - m).astype(BF16)
        pv = jnp.dot(p, vaug_ref[:n_keys, :], preferred_element_type=F32)
        o_ref[0, n * tq:(n + 1) * tq, :] = (pv[:, :HEAD_DIM] / pv[:, HEAD_DIM:]).astype(o_ref.dtype)


def _dilated_attention(qkv, slopes3, bsz, seq, tq=_DILATED_TQ):
    h = N_HEADS
    assert seq % tq == 0
    kern = functools.partial(_dilated_kernel, seq=seq, branches=DILATED_BRANCHES, tq=tq)
    return pl.pallas_call(
        kern,
        grid=(h, bsz),
        in_specs=[
            pl.BlockSpec((1, 1, LANE), lambda j, b: (j, 0, 0)),
            pl.BlockSpec((1, seq, HEAD_DIM), lambda j, b: (b, 0, j)),
            pl.BlockSpec((1, seq, HEAD_DIM), lambda j, b: (b, 0, h + j)),
            pl.BlockSpec((1, seq, HEAD_DIM), lambda j, b: (b, 0, 2 * h + j)),
        ],
        out_specs=pl.BlockSpec((1, seq, HEAD_DIM), lambda j, b: (b, 0, j)),
        out_shape=jax.ShapeDtypeStruct((bsz, seq, h * HEAD_DIM), BF16),
        scratch_shapes=[
            pltpu.VMEM((tq, seq), F32),
            pltpu.VMEM((seq, HEAD_DIM + LANE), BF16),
        ],
        compiler_params=_params("parallel", "arbitrary"),
        name="dilated_attention",
    )(slopes3, qkv, qkv, qkv)


def _stack_heads(ref, col0, width, heads):
    parts = [ref[0, :, col0 + h * width:col0 + (h + 1) * width] for h in heads]
    return parts[0] if len(parts) == 1 else jnp.concatenate(parts, axis=0)


def _dsa_kernel(slope_ref, rows_ref, kidx_ref, qlat_ref, ckv_ref, o_ref, *, seq, topk, tq, n_groups, **static):
    tiles_per_group = seq // tq // n_groups
    group = pl.program_id(1) // tiles_per_group
    for g in range(n_groups):
        body = functools.partial(_dsa_body, slope_ref, rows_ref, kidx_ref, qlat_ref, ckv_ref, o_ref,
                                 seq=(g + 1) * tiles_per_group * tq, topk=topk, tq=tq, **static)
        pl.when(group == g)(body)


def _dsa_body(slope_ref, rows_ref, kidx_ref, qlat_ref, ckv_ref, o_ref, *, seq, topk, tq, qidx_col, widx_col,
              heads_per_dot):
    t = pl.program_id(1)
    q_pos = t * tq + lax.broadcasted_iota(jnp.int32, (tq, seq), 0)
    k_pos = lax.broadcasted_iota(jnp.int32, (tq, seq), 1)
    causal = k_pos <= q_pos

    kidx = kidx_ref[0, :seq, :]
    w_all = rows_ref[0, :, widx_col:widx_col + LANE].astype(F32) * (IDX_DIM ** -0.5 * IDX_HEADS ** -0.5)
    score = jnp.zeros((tq, seq), F32)
    for h0 in range(0, IDX_HEADS, heads_per_dot):
        heads = range(h0, h0 + heads_per_dot)
        logits = _nt_dot(_stack_heads(rows_ref, qidx_col, IDX_DIM, heads), kidx)
        for j, h in enumerate(heads):
            score = score + w_all[:, h:h + 1] * jnp.maximum(logits[j * tq:(j + 1) * tq], 0.0)
    score = jnp.where(causal, score, NEG_BIG)
    score = jnp.where(score == 0.0, 0.0, score)

    bits = lax.bitcast_convert_type(score, jnp.int32)
    keys = jnp.where(bits < 0, bits ^ jnp.int32(0x7FFFFFFF), bits)

    kf = float(topk)
    int_min = jnp.int32(-(2 ** 31))
    keys_t = keys.T
    n_acc = 8

    def count_ge_t(cand):
        hit = jnp.where(keys_t >= cand, 1.0, 0.0).reshape(n_acc, seq // (8 * n_acc), 8, tq)
        partial = jnp.sum(jnp.sum(hit, axis=1), axis=0)
        return jnp.sum(partial, axis=0, keepdims=True)

    n_zero = count_ge_t(jnp.int32(0))
    thr_t = jnp.where(n_zero >= kf, jnp.int32(0), int_min)
    n_ge_t = jnp.where(n_zero >= kf, n_zero, float(seq))

    def bit_step(i, carry):
        thr_t, n_ge_t = carry
        cand = thr_t | (jnp.int32(1) << (30 - i))
        n_cand = count_ge_t(cand)
        keep = n_cand >= kf
        return jnp.where(keep, cand, thr_t), jnp.where(keep, n_cand, n_ge_t)

    thr_t, n_ge_t = lax.fori_loop(0, 31, bit_step, (thr_t, n_ge_t))
    thr = jnp.broadcast_to(thr_t, (tq, tq)).T[:, :1]

    def exact_k():
        return jnp.where((keys >= thr) & causal, 0.0, NEG_BIG)

    def with_ties():
        gt = keys > thr
        eq = keys == thr
        n_gt = jnp.sum(jnp.where(gt, 1.0, 0.0), axis=-1, keepdims=True)
        need = kf - n_gt

        def cut_step(i, cut):
            cand = cut - (jnp.int32(1) << (n_bits - 1 - i))
            cnt = jnp.sum(jnp.where(eq & (k_pos <= cand), 1.0, 0.0), axis=-1, keepdims=True)
            return jnp.where(cnt >= need, cand, cut)

        n_bits = (seq - 1).bit_length()
        cut = lax.fori_loop(0, n_bits, cut_step, jnp.full((tq, 1), (1 << n_bits) - 1, jnp.int32))
        return jnp.where((gt | (eq & (k_pos <= cut))) & causal, 0.0, NEG_BIG)

    mask_bias = lax.cond(jnp.max(n_ge_t) > kf, with_ties, exact_k)

    k_pos_row = lax.broadcasted_iota(jnp.int32, (1, seq), 1).astype(F32)
    ckv = ckv_ref[0, :seq, :]
    for h0 in range(0, N_HEADS, heads_per_dot):
        heads = range(h0, h0 + heads_per_dot)
        s_all = _nt_dot(_stack_heads(qlat_ref, 0, KV_LATENT, heads), ckv)
        probs, sums = [], []
        for j, h in enumerate(heads):
            slope = slope_ref[h][:, :1]
            s = s_all[j * tq:(j + 1) * tq] + (slope * LOG2E) * k_pos_row + mask_bias
            p = jnp.exp2(s - jnp.max(s, axis=-1, keepdims=True))
            sums.append(jnp.sum(p, axis=-1, keepdims=True))
            probs.append(p.astype(BF16))
        p_all = probs[0] if len(probs) == 1 else jnp.concatenate(probs, axis=0)
        o_all = jnp.dot(p_all, ckv, preferred_element_type=F32)
        for j, h in enumerate(heads):
            o = o_all[j * tq:(j + 1) * tq] / sums[j]
            o_ref[0, :, h * KV_LATENT:(h + 1) * KV_LATENT] = o.astype(o_ref.dtype)


def _dsa_attention(proj, qlat, ckv, slopes3, bsz, seq, topk):
    h = N_HEADS
    width = proj.shape[-1]
    tq = _DSA_TQ
    assert seq % (tq * _DSA_KEY_GROUPS) == 0 and N_HEADS % _DSA_HEADS_PER_DOT == 0
    kern = functools.partial(_dsa_kernel, seq=seq, topk=topk, tq=tq, n_groups=_DSA_KEY_GROUPS,
                             qidx_col=_DSA_COLS["q_idx"], widx_col=_DSA_COLS["w_idx"],
                             heads_per_dot=_DSA_HEADS_PER_DOT)
    return pl.pallas_call(
        kern,
        grid=(bsz, seq // tq),
        in_specs=[
            pl.BlockSpec((h, 1, LANE), lambda b, t: (0, 0, 0)),
            pl.BlockSpec((1, tq, width), lambda b, t: (b, t, 0)),
            pl.BlockSpec((1, seq, IDX_DIM), lambda b, t: (b, 0, _DSA_COLS["k_idx"] // IDX_DIM)),
            pl.BlockSpec((1, tq, h * KV_LATENT), lambda b, t: (b, t, 0)),
            pl.BlockSpec((1, seq, KV_LATENT), lambda b, t: (b, 0, 0)),
        ],
        out_specs=pl.BlockSpec((1, tq, h * KV_LATENT), lambda b, t: (b, t, 0)),
        out_shape=jax.ShapeDtypeStruct((bsz, seq, h * KV_LATENT), BF16),
        compiler_params=_params("parallel", "parallel"),
        name="dsa_attention",
    )(slopes3, proj, proj, qlat, ckv)


def _dsa_in_weight(w_in):
    return jnp.pad(w_in.astype(BF16), ((0, 0), (0, _DSA_WIDTH - w_in.shape[1])))


def _dilated_layer(proj, slopes3, bsz, seq):
    return _dilated_attention(proj.reshape(bsz, seq, -1), slopes3, bsz, seq).reshape(bsz * seq, -1)


def _dsa_layer(proj, kv_norm, w_uk, slopes3, bsz, seq, topk):
    qlat, ckv = _qlat_and_ckv(proj, w_uk.astype(BF16), kv_norm, HEAD_DIM ** -0.5 * LOG2E)
    o_lat = _dsa_attention(
        proj.reshape(bsz, seq, -1), qlat.reshape(bsz, seq, -1), ckv.reshape(bsz, seq, -1), slopes3, bsz, seq, topk
    )
    return o_lat.reshape(bsz * seq, -1)


def _forward(x, c, layers, final_norm):
    bsz, seq, d = x.shape
    m = bsz * seq
    topk = min(TOPK_MAX, seq // 4)
    slopes = jnp.exp2(-8.0 * jnp.arange(1, N_HEADS + 1, dtype=F32) / N_HEADS)
    slopes3 = jnp.broadcast_to(slopes[:, None, None], (N_HEADS, 1, LANE))
    xf = x.reshape(m, d)
    for i, (norm_attn, ada_w, ada_b, mixer_w, norm_ffn, w_gate_up, w_down) in enumerate(layers):
        mod3 = _adaln(c, ada_w, ada_b).reshape(bsz * 6, 1, d)
        if i % 2 == 0:
            w_in, w_out = mixer_w
            proj = _mod_matmul(xf, norm_attn, mod3, 0, 1, seq, w_in.astype(BF16), *_TILE_IN)
            o = _dilated_layer(proj, slopes3, bsz, seq)
            xf = _matmul_resid(o, w_out.astype(BF16), xf, mod3, 2, seq, *_TILE_OUT)
        else:
            w_in, kv_norm, w_uk, w_uv, w_out = mixer_w
            proj = _mod_matmul(xf, norm_attn, mod3, 0, 1, seq, _dsa_in_weight(w_in), *_TILE_IN)
            o_lat = _dsa_layer(proj, kv_norm, w_uk, slopes3, bsz, seq, topk)
            xf = _heads_matmul_resid(o_lat, w_uv.astype(BF16), w_out.astype(BF16), xf, mod3, 2, seq, _TILE_OUT[0])
        act = _mod_matmul_swiglu(xf, norm_ffn, mod3, 3, 4, seq, w_gate_up.astype(BF16), *_TILE_GATE_UP)
        xf = _matmul_resid(act, w_down.astype(BF16), xf, mod3, 5, seq, *_TILE_DOWN)
    return _rmsnorm(xf, final_norm, x.dtype).reshape(bsz, seq, d)


def kernel(x, c, l0_norm_attn, l0_ada_w, l0_ada_b, l0_w_in, l0_w_out, l0_norm_ffn, l0_w_gate_up, l0_w_down, l1_norm_attn, l1_ada_w, l1_ada_b, l1_w_in, l1_kv_norm, l1_w_uk, l1_w_uv, l1_w_out, l1_norm_ffn, l1_w_gate_up, l1_w_down, l2_norm_attn, l2_ada_w, l2_ada_b, l2_w_in, l2_w_out, l2_norm_ffn, l2_w_gate_up, l2_w_down, l3_norm_attn, l3_ada_w, l3_ada_b, l3_w_in, l3_kv_norm, l3_w_uk, l3_w_uv, l3_w_out, l3_norm_ffn, l3_w_gate_up, l3_w_down, final_norm):
    layers = (
        (l0_norm_attn, l0_ada_w, l0_ada_b, (l0_w_in, l0_w_out), l0_norm_ffn, l0_w_gate_up, l0_w_down),
        (l1_norm_attn, l1_ada_w, l1_ada_b, (l1_w_in, l1_kv_norm, l1_w_uk, l1_w_uv, l1_w_out),
         l1_norm_ffn, l1_w_gate_up, l1_w_down),
        (l2_norm_attn, l2_ada_w, l2_ada_b, (l2_w_in, l2_w_out), l2_norm_ffn, l2_w_gate_up, l2_w_down),
        (l3_norm_attn, l3_ada_w, l3_ada_b, (l3_w_in, l3_kv_norm, l3_w_uk, l3_w_uv, l3_w_out),
         l3_norm_ffn, l3_w_gate_up, l3_w_down),
    )
    return _forward(x, c, layers, final_norm)
```

```python
import functools

import jax
import jax.numpy as jnp
from jax import lax
from jax.experimental import pallas as pl
from jax.experimental.pallas import tpu as pltpu

N_HEADS = 16
HEAD_DIM = 128
DILATED_BRANCHES = ((128, 1), (512, 4), (2048, 16))
KV_LATENT = 256
IDX_HEADS = 16
IDX_DIM = 128
TOPK_MAX = 256
EPS = 1e-6
NEG_BIG = -1e30
LOG2E = 1.4426950408889634

LANE = 128
BF16_ROWS = 16
VMEM_LIMIT = 56 * 1024 * 1024

F32 = jnp.float32
BF16 = jnp.bfloat16

_TILE_IN = (1024, 1536)
_TILE_GATE_UP = (1024, 512)
_TILE_OUT = (512, 2048)
_TILE_DOWN = (1024, 512)
_TILE_ROWS = 512
_DILATED_TQ = 256
_DSA_TQ = 128
_DSA_KEY_GROUPS = 4
_DSA_HEADS_PER_DOT = 2

_DSA_COLS = {
    "q": 0,
    "c_kv": N_HEADS * HEAD_DIM,
    "q_idx": N_HEADS * HEAD_DIM + KV_LATENT,
    "k_idx": N_HEADS * HEAD_DIM + KV_LATENT + IDX_HEADS * IDX_DIM,
    "w_idx": N_HEADS * HEAD_DIM + KV_LATENT + IDX_HEADS * IDX_DIM + IDX_DIM,
}
assert _DSA_COLS["c_kv"] % KV_LATENT == 0 and all(v % LANE == 0 for v in _DSA_COLS.values())
_DSA_WIDTH = -(-(_DSA_COLS["w_idx"] + IDX_HEADS) // _TILE_IN[1]) * _TILE_IN[1]


def _params(*sem):
    return pltpu.CompilerParams(dimension_semantics=sem, vmem_limit_bytes=VMEM_LIMIT)


def _adaln_kernel(c_ref, w_ref, b_ref, o_ref):
    c = c_ref[...]
    sc = (c * jax.nn.sigmoid(c)).astype(BF16)
    acc = jnp.dot(sc, w_ref[...].astype(BF16), preferred_element_type=F32)
    o_ref[...] = acc + b_ref[...]


def _adaln(c, w, b, tn=1024):
    bsz, d = c.shape
    n = w.shape[1]
    return pl.pallas_call(
        _adaln_kernel,
        grid=(n // tn,),
        in_specs=[
            pl.BlockSpec((bsz, d), lambda j: (0, 0)),
            pl.BlockSpec((d, tn), lambda j: (0, j)),
            pl.BlockSpec((1, tn), lambda j: (0, j)),
        ],
        out_specs=pl.BlockSpec((bsz, tn), lambda j: (0, j)),
        out_shape=jax.ShapeDtypeStruct((bsz, n), F32),
        compiler_params=_params("parallel"),
        name="adaln",
    )(c, w, b.reshape(1, n))


def _modulate_chunk(h_ref, x_ref, g_ref, shift_ref, scale_ref, chunk_rows):
    i, j = pl.program_id(0), pl.program_id(1)
    tm = x_ref.shape[0]
    start = pl.multiple_of(jnp.minimum(j * chunk_rows, tm - chunk_rows), BF16_ROWS)
    rows = pl.ds(start, chunk_rows)
    x = x_ref[rows, :]
    ms = jnp.mean(x * x, axis=-1, keepdims=True)
    col_scale = g_ref[...] * (1.0 + scale_ref[0])
    h_ref[i % 2, rows, :] = (x * lax.rsqrt(ms + EPS) * col_scale + shift_ref[0]).astype(h_ref.dtype)


def _mod_mm_kernel(x_ref, g_ref, shift_ref, scale_ref, w_ref, o_ref, h_ref, *, chunk_rows):
    i = pl.program_id(0)

    @pl.when(i == 0)
    def _():
        _modulate_chunk(h_ref, x_ref, g_ref, shift_ref, scale_ref, chunk_rows)

    @pl.when(i > 0)
    def _():
        _modulate_chunk(h_ref, x_ref, g_ref, shift_ref, scale_ref, chunk_rows)
        o_ref[...] = jnp.dot(h_ref[(i - 1) % 2], w_ref[...], preferred_element_type=F32).astype(o_ref.dtype)


def _mod_mm_swiglu_kernel(x_ref, g_ref, shift_ref, scale_ref, wg_ref, wu_ref, o_ref, h_ref, *, chunk_rows):
    i = pl.program_id(0)

    @pl.when(i == 0)
    def _():
        _modulate_chunk(h_ref, x_ref, g_ref, shift_ref, scale_ref, chunk_rows)

    @pl.when(i > 0)
    def _():
        _modulate_chunk(h_ref, x_ref, g_ref, shift_ref, scale_ref, chunk_rows)
        h = h_ref[(i - 1) % 2]
        g = jnp.dot(h, wg_ref[...], preferred_element_type=F32)
        u = jnp.dot(h, wu_ref[...], preferred_element_type=F32)
        o_ref[...] = (g * jax.nn.sigmoid(g) * u).astype(o_ref.dtype)


def _mod_mm_call(kern, name, x, gain, mod3, shift_k, scale_k, seq, ws, w_col_offs, n, tm, tn):
    m, d = x.shape
    assert m % tm == 0 and n % tn == 0 and seq % tm == 0
    per_b = seq // tm
    n_i, n_j = m // tm, n // tn
    chunk_rows = -(-(-(-tm // n_j)) // BF16_ROWS) * BF16_ROWS
    assert chunk_rows <= tm

    def blk(i):
        return jnp.minimum(i, n_i - 1)

    def col(i, j):
        return jnp.where(i == 0, 0, j)

    w_specs = [pl.BlockSpec((d, tn), lambda i, j, off=off: (0, col(i, j) + off)) for off in w_col_offs]
    return pl.pallas_call(
        functools.partial(kern, chunk_rows=chunk_rows),
        grid=(n_i + 1, n_j),
        in_specs=[
            pl.BlockSpec((tm, d), lambda i, j: (blk(i), 0)),
            pl.BlockSpec((1, d), lambda i, j: (0, 0)),
            pl.BlockSpec((1, 1, d), lambda i, j: ((blk(i) // per_b) * 6 + shift_k, 0, 0)),
            pl.BlockSpec((1, 1, d), lambda i, j: ((blk(i) // per_b) * 6 + scale_k, 0, 0)),
        ] + w_specs,
        out_specs=pl.BlockSpec((tm, tn), lambda i, j: (jnp.maximum(i - 1, 0), col(i, j))),
        out_shape=jax.ShapeDtypeStruct((m, n), BF16),
        scratch_shapes=[pltpu.VMEM((2, tm, d), BF16)],
        compiler_params=_params("arbitrary", "arbitrary"),
        name=name,
    )(x, gain.reshape(1, d), mod3, mod3, *ws)


def _mod_matmul(x, gain, mod3, shift_k, scale_k, seq, w, tm, tn):
    return _mod_mm_call(_mod_mm_kernel, "mod_matmul", x, gain, mod3, shift_k, scale_k, seq, [w], [0],
                        w.shape[1], tm, tn)


def _mod_matmul_swiglu(x, gain, mod3, shift_k, scale_k, seq, w_gate_up, tm, tn):
    f = w_gate_up.shape[1] // 2
    return _mod_mm_call(_mod_mm_swiglu_kernel, "mod_matmul_swiglu", x, gain, mod3, shift_k, scale_k, seq,
                        [w_gate_up, w_gate_up], [0, f // tn], f, tm, tn)


def _rmsnorm_kernel(x_ref, g_ref, o_ref):
    x = x_ref[...].astype(F32)
    ms = jnp.mean(x * x, axis=-1, keepdims=True)
    o_ref[...] = (x * lax.rsqrt(ms + EPS) * g_ref[...]).astype(o_ref.dtype)


def _rmsnorm(x, gain, out_dtype, tm=_TILE_ROWS):
    m, d = x.shape
    return pl.pallas_call(
        _rmsnorm_kernel,
        grid=(m // tm,),
        in_specs=[
            pl.BlockSpec((tm, d), lambda i: (i, 0)),
            pl.BlockSpec((1, d), lambda i: (0, 0)),
        ],
        out_specs=pl.BlockSpec((tm, d), lambda i: (i, 0)),
        out_shape=jax.ShapeDtypeStruct((m, d), out_dtype),
        compiler_params=_params("parallel"),
        name="rmsnorm",
    )(x, gain.reshape(1, d))


def _mm_resid_kernel(a_ref, w_ref, x_ref, gate_ref, o_ref):
    acc = jnp.dot(a_ref[...], w_ref[...], preferred_element_type=F32)
    o_ref[...] = x_ref[...] + gate_ref[0] * acc


def _matmul_resid(a, w, x, mod3, gate_k, seq, tm, tn):
    m, kd = a.shape
    n = w.shape[1]
    assert m % tm == 0 and n % tn == 0 and seq % tm == 0
    per_b = seq // tm
    return pl.pallas_call(
        _mm_resid_kernel,
        grid=(m // tm, n // tn),
        in_specs=[
            pl.BlockSpec((tm, kd), lambda i, j: (i, 0)),
            pl.BlockSpec((kd, tn), lambda i, j: (0, j)),
            pl.BlockSpec((tm, tn), lambda i, j: (i, j)),
            pl.BlockSpec((1, 1, tn), lambda i, j: ((i // per_b) * 6 + gate_k, 0, j)),
        ],
        out_specs=pl.BlockSpec((tm, tn), lambda i, j: (i, j)),
        out_shape=jax.ShapeDtypeStruct((m, n), F32),
        compiler_params=_params("parallel", "parallel"),
        name="matmul_resid",
    )(a, w, x, mod3)


def _head_project(a_ref, w_ref, scale=1.0):
    n_heads, din, dout = w_ref.shape
    parts = []
    for h in range(n_heads):
        acc = jnp.dot(a_ref[:, h * din:(h + 1) * din], w_ref[h], preferred_element_type=F32)
        parts.append((acc * scale).astype(BF16) if scale != 1.0 else acc.astype(BF16))
    return jnp.concatenate(parts, axis=1)


def _heads_mm_resid_kernel(a_ref, wh_ref, w_ref, x_ref, gate_ref, o_ref):
    acc = jnp.dot(_head_project(a_ref, wh_ref), w_ref[...], preferred_element_type=F32)
    o_ref[...] = x_ref[...] + gate_ref[0] * acc


def _heads_matmul_resid(a, w_heads, w, x, mod3, gate_k, seq, tm):
    m = a.shape[0]
    kd, n = w.shape
    assert m % tm == 0 and seq % tm == 0 and w_heads.shape[0] * w_heads.shape[2] == kd
    per_b = seq // tm
    return pl.pallas_call(
        _heads_mm_resid_kernel,
        grid=(m // tm,),
        in_specs=[
            pl.BlockSpec((tm, a.shape[1]), lambda i: (i, 0)),
            pl.BlockSpec(w_heads.shape, lambda i: (0, 0, 0)),
            pl.BlockSpec((kd, n), lambda i: (0, 0)),
            pl.BlockSpec((tm, n), lambda i: (i, 0)),
            pl.BlockSpec((1, 1, n), lambda i: ((i // per_b) * 6 + gate_k, 0, 0)),
        ],
        out_specs=pl.BlockSpec((tm, n), lambda i: (i, 0)),
        out_shape=jax.ShapeDtypeStruct((m, n), F32),
        compiler_params=_params("parallel"),
        name="heads_matmul_resid",
    )(a, w_heads, w, x, mod3)


def _qlat_ckv_kernel(q_ref, ckv_ref, w_ref, g_ref, qlat_ref, ckvn_ref, *, scale):
    qlat_ref[...] = _head_project(q_ref, w_ref, scale)
    c = ckv_ref[...].astype(F32)
    ms = jnp.mean(c * c, axis=-1, keepdims=True)
    ckvn_ref[...] = (c * lax.rsqrt(ms + EPS) * g_ref[...]).astype(ckvn_ref.dtype)


def _qlat_and_ckv(proj, w_uk, kv_norm, scale, tm=2 * _TILE_ROWS):
    m = proj.shape[0]
    h, din, dout = w_uk.shape
    c = kv_norm.shape[0]
    return pl.pallas_call(
        functools.partial(_qlat_ckv_kernel, scale=scale),
        grid=(m // tm,),
        in_specs=[
            pl.BlockSpec((tm, h * din), lambda i: (i, _DSA_COLS["q"] // (h * din))),
            pl.BlockSpec((tm, c), lambda i: (i, _DSA_COLS["c_kv"] // c)),
            pl.BlockSpec((h, din, dout), lambda i: (0, 0, 0)),
            pl.BlockSpec((1, c), lambda i: (0, 0)),
        ],
        out_specs=[
            pl.BlockSpec((tm, h * dout), lambda i: (i, 0)),
            pl.BlockSpec((tm, c), lambda i: (i, 0)),
        ],
        out_shape=[jax.ShapeDtypeStruct((m, h * dout), BF16), jax.ShapeDtypeStruct((m, c), BF16)],
        compiler_params=_params("parallel"),
        name="qlat_ckv",
    )(proj, proj, w_uk, kv_norm.reshape(1, c))


def _nt_dot(a, b):
    return lax.dot_general(a, b, (((1,), (1,)), ((), ())), preferred_element_type=F32)


def _dilated_kernel(slope_ref, q_ref, k_ref, v_ref, o_ref, bias_ref, vaug_ref, *, seq, branches, tq):
    n_tiles = seq // tq

    @pl.when(pl.program_id(1) == 0)
    def _():
        slope = slope_ref[0][:, :1]
        qi = lax.broadcasted_iota(jnp.int32, (tq, seq), 0)
        kj = lax.broadcasted_iota(jnp.int32, (tq, seq), 1)
        d = qi + (seq - tq) - kj
        count = jnp.zeros((tq, seq), F32)
        for window, dil in branches:
            assert dil & (dil - 1) == 0
            count = count + jnp.where(((d & (dil - 1)) == 0) & (d <= window), 1.0, 0.0)
        bias = jnp.log2(jnp.maximum(count, 1.0)) - (slope * LOG2E) * d.astype(F32)
        bias_ref[...] = jnp.where((d >= 0) & (count > 0.0), bias, NEG_BIG)
        vaug_ref[:, HEAD_DIM:] = jnp.ones((seq, LANE), BF16)

    vaug_ref[:, :HEAD_DIM] = v_ref[0]

    for n in range(n_tiles):
        n_keys = (n + 1) * tq
        q = (q_ref[0, n * tq:(n + 1) * tq, :].astype(F32) * (HEAD_DIM ** -0.5 * LOG2E)).astype(BF16)
        s = _nt_dot(q, k_ref[0, :n_keys, :]) + bias_ref[:, seq - n_keys:]
        m = jnp.max(s, axis=-1, keepdims=True)
        p = jnp.exp2(s - m).astype(BF16)
        pv = jnp.dot(p, vaug_ref[:n_keys, :], preferred_element_type=F32)
        o_ref[0, n * tq:(n + 1) * tq, :] = (pv[:, :HEAD_DIM] / pv[:, HEAD_DIM:]).astype(o_ref.dtype)


def _dilated_attention(qkv, slopes3, bsz, seq, tq=_DILATED_TQ):
    h = N_HEADS
    assert seq % tq == 0
    kern = functools.partial(_dilated_kernel, seq=seq, branches=DILATED_BRANCHES, tq=tq)
    return pl.pallas_call(
        kern,
        grid=(h, bsz),
        in_specs=[
            pl.BlockSpec((1, 1, LANE), lambda j, b: (j, 0, 0)),
            pl.BlockSpec((1, seq, HEAD_DIM), lambda j, b: (b, 0, j)),
            pl.BlockSpec((1, seq, HEAD_DIM), lambda j, b: (b, 0, h + j)),
            pl.BlockSpec((1, seq, HEAD_DIM), lambda j, b: (b, 0, 2 * h + j)),
        ],
        out_specs=pl.BlockSpec((1, seq, HEAD_DIM), lambda j, b: (b, 0, j)),
        out_shape=jax.ShapeDtypeStruct((bsz, seq, h * HEAD_DIM), BF16),
        scratch_shapes=[
            pltpu.VMEM((tq, seq), F32),
            pltpu.VMEM((seq, HEAD_DIM + LANE), BF16),
        ],
        compiler_params=_params("parallel", "arbitrary"),
        name="dilated_attention",
    )(slopes3, qkv, qkv, qkv)


def _stack_heads(ref, col0, width, heads):
    parts = [ref[0, :, col0 + h * width:col0 + (h + 1) * width] for h in heads]
    return parts[0] if len(parts) == 1 else jnp.concatenate(parts, axis=0)


def _dsa_kernel(slope_ref, rows_ref, kidx_ref, qlat_ref, ckv_ref, o_ref, *, seq, topk, tq, n_groups, **static):
    tiles_per_group = seq // tq // n_groups
    group = pl.program_id(1) // tiles_per_group
    for g in range(n_groups):
        body = functools.partial(_dsa_body, slope_ref, rows_ref, kidx_ref, qlat_ref, ckv_ref, o_ref,
                                 seq=(g + 1) * tiles_per_group * tq, topk=topk, tq=tq, **static)
        pl.when(group == g)(body)


def _dsa_body(slope_ref, rows_ref, kidx_ref, qlat_ref, ckv_ref, o_ref, *, seq, topk, tq, qidx_col, widx_col,
              heads_per_dot):
    t = pl.program_id(1)
    q_pos = t * tq + lax.broadcasted_iota(jnp.int32, (tq, seq), 0)
    k_pos = lax.broadcasted_iota(jnp.int32, (tq, seq), 1)
    causal = k_pos <= q_pos

    kidx = kidx_ref[0, :seq, :]
    w_all = rows_ref[0, :, widx_col:widx_col + LANE].astype(F32) * (IDX_DIM ** -0.5 * IDX_HEADS ** -0.5)
    score = jnp.zeros((tq, seq), F32)
    for h0 in range(0, IDX_HEADS, heads_per_dot):
        heads = range(h0, h0 + heads_per_dot)
        logits = _nt_dot(_stack_heads(rows_ref, qidx_col, IDX_DIM, heads), kidx)
        for j, h in enumerate(heads):
            score = score + w_all[:, h:h + 1] * jnp.maximum(logits[j * tq:(j + 1) * tq], 0.0)
    score = jnp.where(causal, score, NEG_BIG)
    score = jnp.where(score == 0.0, 0.0, score)

    bits = lax.bitcast_convert_type(score, jnp.int32)
    keys = jnp.where(bits < 0, bits ^ jnp.int32(0x7FFFFFFF), bits)

    kf = float(topk)
    int_min = jnp.int32(-(2 ** 31))
    keys_t = keys.T
    n_acc = 8

    def count_ge_t(cand):
        hit = jnp.where(keys_t >= cand, 1.0, 0.0).reshape(n_acc, seq // (8 * n_acc), 8, tq)
        partial = jnp.sum(jnp.sum(hit, axis=1), axis=0)
        return jnp.sum(partial, axis=0, keepdims=True)

    n_zero = count_ge_t(jnp.int32(0))
    thr_t = jnp.where(n_zero >= kf, jnp.int32(0), int_min)
    n_ge_t = jnp.where(n_zero >= kf, n_zero, float(seq))

    def bit_step(i, carry):
        thr_t, n_ge_t = carry
        cand = thr_t | (jnp.int32(1) << (30 - i))
        n_cand = count_ge_t(cand)
        keep = n_cand >= kf
        return jnp.where(keep, cand, thr_t), jnp.where(keep, n_cand, n_ge_t)

    all_kept = (t + 1) * tq <= topk
    thr_t, n_ge_t = lax.fori_loop(0, jnp.where(all_kept, 0, 31), bit_step, (thr_t, n_ge_t))
    thr_t = jnp.where(all_kept, int_min, thr_t)
    thr = jnp.broadcast_to(thr_t, (tq, tq)).T[:, :1]

    def exact_k():
        return jnp.where((keys >= thr) & causal, 0.0, NEG_BIG)

    def with_ties():
        gt = keys > thr
        eq = keys == thr
        n_gt = jnp.sum(jnp.where(gt, 1.0, 0.0), axis=-1, keepdims=True)
        need = kf - n_gt

        def cut_step(i, cut):
            cand = cut - (jnp.int32(1) << (n_bits - 1 - i))
            cnt = jnp.sum(jnp.where(eq & (k_pos <= cand), 1.0, 0.0), axis=-1, keepdims=True)
            return jnp.where(cnt >= need, cand, cut)

        n_bits = (seq - 1).bit_length()
        cut = lax.fori_loop(0, n_bits, cut_step, jnp.full((tq, 1), (1 << n_bits) - 1, jnp.int32))
        return jnp.where((gt | (eq & (k_pos <= cut))) & causal, 0.0, NEG_BIG)

    mask_bias = lax.cond(jnp.logical_and(jnp.max(n_ge_t) > kf, jnp.logical_not(all_kept)), with_ties, exact_k)

    k_pos_row = lax.broadcasted_iota(jnp.int32, (1, seq), 1).astype(F32)
    ckv = ckv_ref[0, :seq, :]
    for h0 in range(0, N_HEADS, heads_per_dot):
        heads = range(h0, h0 + heads_per_dot)
        s_all = _nt_dot(_stack_heads(qlat_ref, 0, KV_LATENT, heads), ckv)
        probs, sums = [], []
        for j, h in enumerate(heads):
            slope = slope_ref[h][:, :1]
            s = s_all[j * tq:(j + 1) * tq] + (slope * LOG2E) * k_pos_row + mask_bias
            p = jnp.exp2(s - jnp.max(s, axis=-1, keepdims=True))
            sums.append(jnp.sum(p, axis=-1, keepdims=True))
            probs.append(p.astype(BF16))
        p_all = probs[0] if len(probs) == 1 else jnp.concatenate(probs, axis=0)
        o_all = jnp.dot(p_all, ckv, preferred_element_type=F32)
        for j, h in enumerate(heads):
            o = o_all[j * tq:(j + 1) * tq] / sums[j]
            o_ref[0, :, h * KV_LATENT:(h + 1) * KV_LATENT] = o.astype(o_ref.dtype)


def _dsa_attention(proj, qlat, ckv, slopes3, bsz, seq, topk):
    h = N_HEADS
    width = proj.shape[-1]
    tq = _DSA_TQ
    assert seq % (tq * _DSA_KEY_GROUPS) == 0 and N_HEADS % _DSA_HEADS_PER_DOT == 0
    kern = functools.partial(_dsa_kernel, seq=seq, topk=topk, tq=tq, n_groups=_DSA_KEY_GROUPS,
                             qidx_col=_DSA_COLS["q_idx"], widx_col=_DSA_COLS["w_idx"],
                             heads_per_dot=_DSA_HEADS_PER_DOT)
    return pl.pallas_call(
        kern,
        grid=(bsz, seq // tq),
        in_specs=[
            pl.BlockSpec((h, 1, LANE), lambda b, t: (0, 0, 0)),
            pl.BlockSpec((1, tq, width), lambda b, t: (b, t, 0)),
            pl.BlockSpec((1, seq, IDX_DIM), lambda b, t: (b, 0, _DSA_COLS["k_idx"] // IDX_DIM)),
            pl.BlockSpec((1, tq, h * KV_LATENT), lambda b, t: (b, t, 0)),
            pl.BlockSpec((1, seq, KV_LATENT), lambda b, t: (b, 0, 0)),
        ],
        out_specs=pl.BlockSpec((1, tq, h * KV_LATENT), lambda b, t: (b, t, 0)),
        out_shape=jax.ShapeDtypeStruct((bsz, seq, h * KV_LATENT), BF16),
        compiler_params=_params("parallel", "parallel"),
        name="dsa_attention",
    )(slopes3, proj, proj, qlat, ckv)


def _dsa_in_weight(w_in):
    return jnp.pad(w_in.astype(BF16), ((0, 0), (0, _DSA_WIDTH - w_in.shape[1])))


def _dilated_layer(proj, slopes3, bsz, seq):
    return _dilated_attention(proj.reshape(bsz, seq, -1), slopes3, bsz, seq).reshape(bsz * seq, -1)


def _dsa_layer(proj, kv_norm, w_uk, slopes3, bsz, seq, topk):
    qlat, ckv = _qlat_and_ckv(proj, w_uk.astype(BF16), kv_norm, HEAD_DIM ** -0.5 * LOG2E)
    o_lat = _dsa_attention(
        proj.reshape(bsz, seq, -1), qlat.reshape(bsz, seq, -1), ckv.reshape(bsz, seq, -1), slopes3, bsz, seq, topk
    )
    return o_lat.reshape(bsz * seq, -1)


def _forward(x, c, layers, final_norm):
    bsz, seq, d = x.shape
    m = bsz * seq
    topk = min(TOPK_MAX, seq // 4)
    slopes = jnp.exp2(-8.0 * jnp.arange(1, N_HEADS + 1, dtype=F32) / N_HEADS)
    slopes3 = jnp.broadcast_to(slopes[:, None, None], (N_HEADS, 1, LANE))
    xf = x.reshape(m, d)
    for i, (norm_attn, ada_w, ada_b, mixer_w, norm_ffn, w_gate_up, w_down) in enumerate(layers):
        mod3 = _adaln(c, ada_w, ada_b).reshape(bsz * 6, 1, d)
        if i % 2 == 0:
            w_in, w_out = mixer_w
            proj = _mod_matmul(xf, norm_attn, mod3, 0, 1, seq, w_in.astype(BF16), *_TILE_IN)
            o = _dilated_layer(proj, slopes3, bsz, seq)
            xf = _matmul_resid(o, w_out.astype(BF16), xf, mod3, 2, seq, *_TILE_OUT)
        else:
            w_in, kv_norm, w_uk, w_uv, w_out = mixer_w
            proj = _mod_matmul(xf, norm_attn, mod3, 0, 1, seq, _dsa_in_weight(w_in), *_TILE_IN)
            o_lat = _dsa_layer(proj, kv_norm, w_uk, slopes3, bsz, seq, topk)
            xf = _heads_matmul_resid(o_lat, w_uv.astype(BF16), w_out.astype(BF16), xf, mod3, 2, seq, _TILE_OUT[0])
        act = _mod_matmul_swiglu(xf, norm_ffn, mod3, 3, 4, seq, w_gate_up.astype(BF16), *_TILE_GATE_UP)
        xf = _matmul_resid(act, w_down.astype(BF16), xf, mod3, 5, seq, *_TILE_DOWN)
    return _rmsnorm(xf, final_norm, x.dtype).reshape(bsz, seq, d)


def kernel(x, c, l0_norm_attn, l0_ada_w, l0_ada_b, l0_w_in, l0_w_out, l0_norm_ffn, l0_w_gate_up, l0_w_down, l1_norm_attn, l1_ada_w, l1_ada_b, l1_w_in, l1_kv_norm, l1_w_uk, l1_w_uv, l1_w_out, l1_norm_ffn, l1_w_gate_up, l1_w_down, l2_norm_attn, l2_ada_w, l2_ada_b, l2_w_in, l2_w_out, l2_norm_ffn, l2_w_gate_up, l2_w_down, l3_norm_attn, l3_ada_w, l3_ada_b, l3_w_in, l3_kv_norm, l3_w_uk, l3_w_uv, l3_w_out, l3_norm_ffn, l3_w_gate_up, l3_w_down, final_norm):
    layers = (
        (l0_norm_attn, l0_ada_w, l0_ada_b, (l0_w_in, l0_w_out), l0_norm_ffn, l0_w_gate_up, l0_w_down),
        (l1_norm_attn, l1_ada_w, l1_ada_b, (l1_w_in, l1_kv_norm, l1_w_uk, l1_w_uv, l1_w_out),
         l1_norm_ffn, l1_w_gate_up, l1_w_down),
        (l2_norm_attn, l2_ada_w, l2_ada_b, (l2_w_in, l2_w_out), l2_norm_ffn, l2_w_gate_up, l2_w_down),
        (l3_norm_attn, l3_ada_w, l3_ada_b, (l3_w_in, l3_kv_norm, l3_w_uk, l3_w_uv, l3_w_out),
         l3_norm_ffn, l3_w_gate_up, l3_w_down),
    )
    return _forward(x, c, layers, final_norm)
```

```python
import functools

import jax
import jax.numpy as jnp
from jax import lax
from jax.experimental import pallas as pl
from jax.experimental.pallas import tpu as pltpu

N_HEADS = 16
HEAD_DIM = 128
DILATED_BRANCHES = ((128, 1), (512, 4), (2048, 16))
KV_LATENT = 256
IDX_HEADS = 16
IDX_DIM = 128
TOPK_MAX = 256
EPS = 1e-6
NEG_BIG = -1e30
LOG2E = 1.4426950408889634

LANE = 128
BF16_ROWS = 16
VMEM_LIMIT = 56 * 1024 * 1024

F32 = jnp.float32
BF16 = jnp.bfloat16

_TILE_IN = (1024, 1536)
_TILE_GATE_UP = (1024, 512)
_TILE_OUT = (512, 2048)
_TILE_DOWN = (1024, 512)
_TILE_ROWS = 512
_DILATED_TQ = 256
_DSA_TQ = 128
_DSA_KEY_GROUPS = 4
_DSA_HEADS_PER_DOT = 2

_DSA_COLS = {
    "q": 0,
    "c_kv": N_HEADS * HEAD_DIM,
    "q_idx": N_HEADS * HEAD_DIM + KV_LATENT,
    "k_idx": N_HEADS * HEAD_DIM + KV_LATENT + IDX_HEADS * IDX_DIM,
    "w_idx": N_HEADS * HEAD_DIM + KV_LATENT + IDX_HEADS * IDX_DIM + IDX_DIM,
}
assert _DSA_COLS["c_kv"] % KV_LATENT == 0 and all(v % LANE == 0 for v in _DSA_COLS.values())
_DSA_WIDTH = -(-(_DSA_COLS["w_idx"] + IDX_HEADS) // _TILE_IN[1]) * _TILE_IN[1]


def _params(*sem):
    return pltpu.CompilerParams(dimension_semantics=sem, vmem_limit_bytes=VMEM_LIMIT)


def _adaln_kernel(c_ref, w_ref, b_ref, o_ref):
    c = c_ref[...]
    sc = (c * jax.nn.sigmoid(c)).astype(BF16)
    acc = jnp.dot(sc, w_ref[...].astype(BF16), preferred_element_type=F32)
    o_ref[...] = acc + b_ref[...]


def _adaln(c, w, b, tn=1024):
    bsz, d = c.shape
    n = w.shape[1]
    return pl.pallas_call(
        _adaln_kernel,
        grid=(n // tn,),
        in_specs=[
            pl.BlockSpec((bsz, d), lambda j: (0, 0)),
            pl.BlockSpec((d, tn), lambda j: (0, j)),
            pl.BlockSpec((1, tn), lambda j: (0, j)),
        ],
        out_specs=pl.BlockSpec((bsz, tn), lambda j: (0, j)),
        out_shape=jax.ShapeDtypeStruct((bsz, n), F32),
        compiler_params=_params("parallel"),
        name="adaln",
    )(c, w, b.reshape(1, n))


def _modulate_chunk(h_ref, x_ref, g_ref, shift_ref, scale_ref, chunk_rows):
    i, j = pl.program_id(0), pl.program_id(1)
    tm = x_ref.shape[0]
    start = pl.multiple_of(jnp.minimum(j * chunk_rows, tm - chunk_rows), BF16_ROWS)
    rows = pl.ds(start, chunk_rows)
    x = x_ref[rows, :]
    ms = jnp.mean(x * x, axis=-1, keepdims=True)
    col_scale = g_ref[...] * (1.0 + scale_ref[0])
    normed = (x * lax.rsqrt(ms + EPS)).astype(h_ref.dtype)
    h_ref[i % 2, rows, :] = normed * col_scale.astype(h_ref.dtype) + shift_ref[0].astype(h_ref.dtype)


def _mod_mm_kernel(x_ref, g_ref, shift_ref, scale_ref, w_ref, o_ref, h_ref, *, chunk_rows):
    i = pl.program_id(0)

    @pl.when(i == 0)
    def _():
        _modulate_chunk(h_ref, x_ref, g_ref, shift_ref, scale_ref, chunk_rows)

    @pl.when(i > 0)
    def _():
        _modulate_chunk(h_ref, x_ref, g_ref, shift_ref, scale_ref, chunk_rows)
        o_ref[...] = jnp.dot(h_ref[(i - 1) % 2], w_ref[...], preferred_element_type=F32).astype(o_ref.dtype)


def _mod_mm_swiglu_kernel(x_ref, g_ref, shift_ref, scale_ref, wg_ref, wu_ref, o_ref, h_ref, *, chunk_rows):
    i = pl.program_id(0)

    @pl.when(i == 0)
    def _():
        _modulate_chunk(h_ref, x_ref, g_ref, shift_ref, scale_ref, chunk_rows)

    @pl.when(i > 0)
    def _():
        _modulate_chunk(h_ref, x_ref, g_ref, shift_ref, scale_ref, chunk_rows)
        h = h_ref[(i - 1) % 2]
        g = jnp.dot(h, wg_ref[...], preferred_element_type=F32)
        u = jnp.dot(h, wu_ref[...], preferred_element_type=F32)
        o_ref[...] = (g * jax.nn.sigmoid(g) * u).astype(o_ref.dtype)


def _mod_mm_call(kern, name, x, gain, mod3, shift_k, scale_k, seq, ws, w_col_offs, n, tm, tn):
    m, d = x.shape
    assert m % tm == 0 and n % tn == 0 and seq % tm == 0
    per_b = seq // tm
    n_i, n_j = m // tm, n // tn
    chunk_rows = -(-(-(-tm // n_j)) // BF16_ROWS) * BF16_ROWS
    assert chunk_rows <= tm

    def blk(i):
        return jnp.minimum(i, n_i - 1)

    def col(i, j):
        return jnp.where(i == 0, 0, j)

    w_specs = [pl.BlockSpec((d, tn), lambda i, j, off=off: (0, col(i, j) + off)) for off in w_col_offs]
    return pl.pallas_call(
        functools.partial(kern, chunk_rows=chunk_rows),
        grid=(n_i + 1, n_j),
        in_specs=[
            pl.BlockSpec((tm, d), lambda i, j: (blk(i), 0)),
            pl.BlockSpec((1, d), lambda i, j: (0, 0)),
            pl.BlockSpec((1, 1, d), lambda i, j: ((blk(i) // per_b) * 6 + shift_k, 0, 0)),
            pl.BlockSpec((1, 1, d), lambda i, j: ((blk(i) // per_b) * 6 + scale_k, 0, 0)),
        ] + w_specs,
        out_specs=pl.BlockSpec((tm, tn), lambda i, j: (jnp.maximum(i - 1, 0), col(i, j))),
        out_shape=jax.ShapeDtypeStruct((m, n), BF16),
        scratch_shapes=[pltpu.VMEM((2, tm, d), BF16)],
        compiler_params=_params("arbitrary", "arbitrary"),
        name=name,
    )(x, gain.reshape(1, d), mod3, mod3, *ws)


def _mod_matmul(x, gain, mod3, shift_k, scale_k, seq, w, tm, tn):
    return _mod_mm_call(_mod_mm_kernel, "mod_matmul", x, gain, mod3, shift_k, scale_k, seq, [w], [0],
                        w.shape[1], tm, tn)


def _mod_matmul_swiglu(x, gain, mod3, shift_k, scale_k, seq, w_gate_up, tm, tn):
    f = w_gate_up.shape[1] // 2
    return _mod_mm_call(_mod_mm_swiglu_kernel, "mod_matmul_swiglu", x, gain, mod3, shift_k, scale_k, seq,
                        [w_gate_up, w_gate_up], [0, f // tn], f, tm, tn)


def _rmsnorm_kernel(x_ref, g_ref, o_ref):
    x = x_ref[...].astype(F32)
    ms = jnp.mean(x * x, axis=-1, keepdims=True)
    o_ref[...] = (x * lax.rsqrt(ms + EPS) * g_ref[...]).astype(o_ref.dtype)


def _rmsnorm(x, gain, out_dtype, tm=_TILE_ROWS):
    m, d = x.shape
    return pl.pallas_call(
        _rmsnorm_kernel,
        grid=(m // tm,),
        in_specs=[
            pl.BlockSpec((tm, d), lambda i: (i, 0)),
            pl.BlockSpec((1, d), lambda i: (0, 0)),
        ],
        out_specs=pl.BlockSpec((tm, d), lambda i: (i, 0)),
        out_shape=jax.ShapeDtypeStruct((m, d), out_dtype),
        compiler_params=_params("parallel"),
        name="rmsnorm",
    )(x, gain.reshape(1, d))


def _mm_resid_kernel(a_ref, w_ref, x_ref, gate_ref, o_ref):
    acc = jnp.dot(a_ref[...], w_ref[...], preferred_element_type=F32)
    o_ref[...] = x_ref[...] + gate_ref[0] * acc


def _matmul_resid(a, w, x, mod3, gate_k, seq, tm, tn):
    m, kd = a.shape
    n = w.shape[1]
    assert m % tm == 0 and n % tn == 0 and seq % tm == 0
    per_b = seq // tm
    return pl.pallas_call(
        _mm_resid_kernel,
        grid=(m // tm, n // tn),
        in_specs=[
            pl.BlockSpec((tm, kd), lambda i, j: (i, 0)),
            pl.BlockSpec((kd, tn), lambda i, j: (0, j)),
            pl.BlockSpec((tm, tn), lambda i, j: (i, j)),
            pl.BlockSpec((1, 1, tn), lambda i, j: ((i // per_b) * 6 + gate_k, 0, j)),
        ],
        out_specs=pl.BlockSpec((tm, tn), lambda i, j: (i, j)),
        out_shape=jax.ShapeDtypeStruct((m, n), F32),
        compiler_params=_params("parallel", "parallel"),
        name="matmul_resid",
    )(a, w, x, mod3)


def _head_project(a_ref, w_ref, scale=1.0):
    n_heads, din, dout = w_ref.shape
    parts = []
    for h in range(n_heads):
        acc = jnp.dot(a_ref[:, h * din:(h + 1) * din], w_ref[h], preferred_element_type=F32)
        parts.append((acc * scale).astype(BF16) if scale != 1.0 else acc.astype(BF16))
    return jnp.concatenate(parts, axis=1)


def _heads_mm_resid_kernel(a_ref, wh_ref, w_ref, x_ref, gate_ref, o_ref):
    acc = jnp.dot(_head_project(a_ref, wh_ref), w_ref[...], preferred_element_type=F32)
    o_ref[...] = x_ref[...] + gate_ref[0] * acc


def _heads_matmul_resid(a, w_heads, w, x, mod3, gate_k, seq, tm):
    m = a.shape[0]
    kd, n = w.shape
    assert m % tm == 0 and seq % tm == 0 and w_heads.shape[0] * w_heads.shape[2] == kd
    per_b = seq // tm
    return pl.pallas_call(
        _heads_mm_resid_kernel,
        grid=(m // tm,),
        in_specs=[
            pl.BlockSpec((tm, a.shape[1]), lambda i: (i, 0)),
            pl.BlockSpec(w_heads.shape, lambda i: (0, 0, 0)),
            pl.BlockSpec((kd, n), lambda i: (0, 0)),
            pl.BlockSpec((tm, n), lambda i: (i, 0)),
            pl.BlockSpec((1, 1, n), lambda i: ((i // per_b) * 6 + gate_k, 0, 0)),
        ],
        out_specs=pl.BlockSpec((tm, n), lambda i: (i, 0)),
        out_shape=jax.ShapeDtypeStruct((m, n), F32),
        compiler_params=_params("parallel"),
        name="heads_matmul_resid",
    )(a, w_heads, w, x, mod3)


def _qlat_ckv_kernel(q_ref, ckv_ref, w_ref, g_ref, qlat_ref, ckvn_ref, *, scale):
    qlat_ref[...] = _head_project(q_ref, w_ref, scale)
    c = ckv_ref[...].astype(F32)
    ms = jnp.mean(c * c, axis=-1, keepdims=True)
    ckvn_ref[...] = (c * lax.rsqrt(ms + EPS) * g_ref[...]).astype(ckvn_ref.dtype)


def _qlat_and_ckv(proj, w_uk, kv_norm, scale, tm=2 * _TILE_ROWS):
    m = proj.shape[0]
    h, din, dout = w_uk.shape
    c = kv_norm.shape[0]
    return pl.pallas_call(
        functools.partial(_qlat_ckv_kernel, scale=scale),
        grid=(m // tm,),
        in_specs=[
            pl.BlockSpec((tm, h * din), lambda i: (i, _DSA_COLS["q"] // (h * din))),
            pl.BlockSpec((tm, c), lambda i: (i, _DSA_COLS["c_kv"] // c)),
            pl.BlockSpec((h, din, dout), lambda i: (0, 0, 0)),
            pl.BlockSpec((1, c), lambda i: (0, 0)),
        ],
        out_specs=[
            pl.BlockSpec((tm, h * dout), lambda i: (i, 0)),
            pl.BlockSpec((tm, c), lambda i: (i, 0)),
        ],
        out_shape=[jax.ShapeDtypeStruct((m, h * dout), BF16), jax.ShapeDtypeStruct((m, c), BF16)],
        compiler_params=_params("parallel"),
        name="qlat_ckv",
    )(proj, proj, w_uk, kv_norm.reshape(1, c))


def _nt_dot(a, b):
    return lax.dot_general(a, b, (((1,), (1,)), ((), ())), preferred_element_type=F32)


def _dilated_kernel(slope_ref, q_ref, k_ref, v_ref, o_ref, bias_ref, vaug_ref, *, seq, branches, tq):
    n_tiles = seq // tq

    @pl.when(pl.program_id(1) == 0)
    def _():
        slope = slope_ref[0][:, :1]
        qi = lax.broadcasted_iota(jnp.int32, (tq, seq), 0)
        kj = lax.broadcasted_iota(jnp.int32, (tq, seq), 1)
        d = qi + (seq - tq) - kj
        count = jnp.zeros((tq, seq), F32)
        for window, dil in branches:
            assert dil & (dil - 1) == 0
            count = count + jnp.where(((d & (dil - 1)) == 0) & (d <= window), 1.0, 0.0)
        bias = jnp.log2(jnp.maximum(count, 1.0)) - (slope * LOG2E) * d.astype(F32)
        bias_ref[...] = jnp.where((d >= 0) & (count > 0.0), bias, NEG_BIG)
        vaug_ref[:, HEAD_DIM:] = jnp.ones((seq, LANE), BF16)

    vaug_ref[:, :HEAD_DIM] = v_ref[0]

    for n in range(n_tiles):
        n_keys = (n + 1) * tq
        q = (q_ref[0, n * tq:(n + 1) * tq, :].astype(F32) * (HEAD_DIM ** -0.5 * LOG2E)).astype(BF16)
        s = _nt_dot(q, k_ref[0, :n_keys, :]) + bias_ref[:, seq - n_keys:]
        m = jnp.max(s, axis=-1, keepdims=True)
        p = jnp.exp2(s - m).astype(BF16)
        pv = jnp.dot(p, vaug_ref[:n_keys, :], preferred_element_type=F32)
        o_ref[0, n * tq:(n + 1) * tq, :] = (pv[:, :HEAD_DIM] / pv[:, HEAD_DIM:]).astype(o_ref.dtype)


def _dilated_attention(qkv, slopes3, bsz, seq, tq=_DILATED_TQ):
    h = N_HEADS
    assert seq % tq == 0
    kern = functools.partial(_dilated_kernel, seq=seq, branches=DILATED_BRANCHES, tq=tq)
    return pl.pallas_call(
        kern,
        grid=(h, bsz),
        in_specs=[
            pl.BlockSpec((1, 1, LANE), lambda j, b: (j, 0, 0)),
            pl.BlockSpec((1, seq, HEAD_DIM), lambda j, b: (b, 0, j)),
            pl.BlockSpec((1, seq, HEAD_DIM), lambda j, b: (b, 0, h + j)),
            pl.BlockSpec((1, seq, HEAD_DIM), lambda j, b: (b, 0, 2 * h + j)),
        ],
        out_specs=pl.BlockSpec((1, seq, HEAD_DIM), lambda j, b: (b, 0, j)),
        out_shape=jax.ShapeDtypeStruct((bsz, seq, h * HEAD_DIM), BF16),
        scratch_shapes=[
            pltpu.VMEM((tq, seq), F32),
            pltpu.VMEM((seq, HEAD_DIM + LANE), BF16),
        ],
        compiler_params=_params("parallel", "arbitrary"),
        name="dilated_attention",
    )(slopes3, qkv, qkv, qkv)


def _stack_heads(ref, col0, width, heads):
    parts = [ref[0, :, col0 + h * width:col0 + (h + 1) * width] for h in heads]
    return parts[0] if len(parts) == 1 else jnp.concatenate(parts, axis=0)


def _dsa_kernel(slope_ref, rows_ref, kidx_ref, qlat_ref, ckv_ref, o_ref, *, seq, topk, tq, n_groups, **static):
    tiles_per_group = seq // tq // n_groups
    group = pl.program_id(1) // tiles_per_group
    for g in range(n_groups):
        body = functools.partial(_dsa_body, slope_ref, rows_ref, kidx_ref, qlat_ref, ckv_ref, o_ref,
                                 seq=(g + 1) * tiles_per_group * tq, topk=topk, tq=tq, **static)
        pl.when(group == g)(body)


def _dsa_body(slope_ref, rows_ref, kidx_ref, qlat_ref, ckv_ref, o_ref, *, seq, topk, tq, qidx_col, widx_col,
              heads_per_dot):
    t = pl.program_id(1)
    q_pos = t * tq + lax.broadcasted_iota(jnp.int32, (tq, seq), 0)
    k_pos = lax.broadcasted_iota(jnp.int32, (tq, seq), 1)
    causal = k_pos <= q_pos

    kidx = kidx_ref[0, :seq, :]
    w_all = rows_ref[0, :, widx_col:widx_col + LANE].astype(F32) * (IDX_DIM ** -0.5 * IDX_HEADS ** -0.5)
    score = jnp.zeros((tq, seq), F32)
    for h0 in range(0, IDX_HEADS, heads_per_dot):
        heads = range(h0, h0 + heads_per_dot)
        logits = _nt_dot(_stack_heads(rows_ref, qidx_col, IDX_DIM, heads), kidx)
        for j, h in enumerate(heads):
            score = score + w_all[:, h:h + 1] * jnp.maximum(logits[j * tq:(j + 1) * tq], 0.0)
    score = jnp.where(causal, score, NEG_BIG)
    score = jnp.where(score == 0.0, 0.0, score)

    bits = lax.bitcast_convert_type(score, jnp.int32)
    keys = jnp.where(bits < 0, bits ^ jnp.int32(0x7FFFFFFF), bits)

    kf = float(topk)
    int_min = jnp.int32(-(2 ** 31))
    keys_t = keys.T
    n_acc = 8

    def count_ge_t(cand):
        hit = jnp.where(keys_t >= cand, 1.0, 0.0).reshape(n_acc, seq // (8 * n_acc), 8, tq)
        partial = jnp.sum(jnp.sum(hit, axis=1), axis=0)
        return jnp.sum(partial, axis=0, keepdims=True)

    n_zero = count_ge_t(jnp.int32(0))
    thr_t = jnp.where(n_zero >= kf, jnp.int32(0), int_min)
    n_ge_t = jnp.where(n_zero >= kf, n_zero, float(seq))

    def bit_step(i, carry):
        thr_t, n_ge_t = carry
        cand = thr_t | (jnp.int32(1) << (30 - i))
        n_cand = count_ge_t(cand)
        keep = n_cand >= kf
        return jnp.where(keep, cand, thr_t), jnp.where(keep, n_cand, n_ge_t)

    all_kept = (t + 1) * tq <= topk
    thr_t, n_ge_t = lax.fori_loop(0, jnp.where(all_kept, 0, 31), bit_step, (thr_t, n_ge_t))
    thr_t = jnp.where(all_kept, int_min, thr_t)
    thr = jnp.broadcast_to(thr_t, (tq, tq)).T[:, :1]

    def exact_k():
        return jnp.where((keys >= thr) & causal, 0.0, NEG_BIG)

    def with_ties():
        gt = keys > thr
        eq = keys == thr
        n_gt = jnp.sum(jnp.where(gt, 1.0, 0.0), axis=-1, keepdims=True)
        need = kf - n_gt

        def cut_step(i, cut):
            cand = cut - (jnp.int32(1) << (n_bits - 1 - i))
            cnt = jnp.sum(jnp.where(eq & (k_pos <= cand), 1.0, 0.0), axis=-1, keepdims=True)
            return jnp.where(cnt >= need, cand, cut)

        n_bits = (seq - 1).bit_length()
        cut = lax.fori_loop(0, n_bits, cut_step, jnp.full((tq, 1), (1 << n_bits) - 1, jnp.int32))
        return jnp.where((gt | (eq & (k_pos <= cut))) & causal, 0.0, NEG_BIG)

    mask_bias = lax.cond(jnp.logical_and(jnp.max(n_ge_t) > kf, jnp.logical_not(all_kept)), with_ties, exact_k)

    k_pos_row = lax.broadcasted_iota(jnp.int32, (1, seq), 1).astype(F32)
    ckv = ckv_ref[0, :seq, :]
    for h0 in range(0, N_HEADS, heads_per_dot):
        heads = range(h0, h0 + heads_per_dot)
        s_all = _nt_dot(_stack_heads(qlat_ref, 0, KV_LATENT, heads), ckv)
        probs, sums = [], []
        for j, h in enumerate(heads):
            slope = slope_ref[h][:, :1]
            s = s_all[j * tq:(j + 1) * tq] + (slope * LOG2E) * k_pos_row + mask_bias
            p = jnp.exp2(s - jnp.max(s, axis=-1, keepdims=True))
            sums.append(jnp.sum(p, axis=-1, keepdims=True))
            probs.append(p.astype(BF16))
        p_all = probs[0] if len(probs) == 1 else jnp.concatenate(probs, axis=0)
        o_all = jnp.dot(p_all, ckv, preferred_element_type=F32)
        for j, h in enumerate(heads):
            o = o_all[j * tq:(j + 1) * tq] / sums[j]
            o_ref[0, :, h * KV_LATENT:(h + 1) * KV_LATENT] = o.astype(o_ref.dtype)


def _dsa_attention(proj, qlat, ckv, slopes3, bsz, seq, topk):
    h = N_HEADS
    width = proj.shape[-1]
    tq = _DSA_TQ
    assert seq % (tq * _DSA_KEY_GROUPS) == 0 and N_HEADS % _DSA_HEADS_PER_DOT == 0
    kern = functools.partial(_dsa_kernel, seq=seq, topk=topk, tq=tq, n_groups=_DSA_KEY_GROUPS,
                             qidx_col=_DSA_COLS["q_idx"], widx_col=_DSA_COLS["w_idx"],
                             heads_per_dot=_DSA_HEADS_PER_DOT)
    return pl.pallas_call(
        kern,
        grid=(bsz, seq // tq),
        in_specs=[
            pl.BlockSpec((h, 1, LANE), lambda b, t: (0, 0, 0)),
            pl.BlockSpec((1, tq, width), lambda b, t: (b, t, 0)),
            pl.BlockSpec((1, seq, IDX_DIM), lambda b, t: (b, 0, _DSA_COLS["k_idx"] // IDX_DIM)),
            pl.BlockSpec((1, tq, h * KV_LATENT), lambda b, t: (b, t, 0)),
            pl.BlockSpec((1, seq, KV_LATENT), lambda b, t: (b, 0, 0)),
        ],
        out_specs=pl.BlockSpec((1, tq, h * KV_LATENT), lambda b, t: (b, t, 0)),
        out_shape=jax.ShapeDtypeStruct((bsz, seq, h * KV_LATENT), BF16),
        compiler_params=_params("parallel", "parallel"),
        name="dsa_attention",
    )(slopes3, proj, proj, qlat, ckv)


def _dsa_in_weight(w_in):
    return jnp.pad(w_in.astype(BF16), ((0, 0), (0, _DSA_WIDTH - w_in.shape[1])))


def _dilated_layer(proj, slopes3, bsz, seq):
    return _dilated_attention(proj.reshape(bsz, seq, -1), slopes3, bsz, seq).reshape(bsz * seq, -1)


def _dsa_layer(proj, kv_norm, w_uk, slopes3, bsz, seq, topk):
    qlat, ckv = _qlat_and_ckv(proj, w_uk.astype(BF16), kv_norm, HEAD_DIM ** -0.5 * LOG2E)
    o_lat = _dsa_attention(
        proj.reshape(bsz, seq, -1), qlat.reshape(bsz, seq, -1), ckv.reshape(bsz, seq, -1), slopes3, bsz, seq, topk
    )
    return o_lat.reshape(bsz * seq, -1)


def _forward(x, c, layers, final_norm):
    bsz, seq, d = x.shape
    m = bsz * seq
    topk = min(TOPK_MAX, seq // 4)
    slopes = jnp.exp2(-8.0 * jnp.arange(1, N_HEADS + 1, dtype=F32) / N_HEADS)
    slopes3 = jnp.broadcast_to(slopes[:, None, None], (N_HEADS, 1, LANE))
    xf = x.reshape(m, d)
    for i, (norm_attn, ada_w, ada_b, mixer_w, norm_ffn, w_gate_up, w_down) in enumerate(layers):
        mod3 = _adaln(c, ada_w, ada_b).reshape(bsz * 6, 1, d)
        if i % 2 == 0:
            w_in, w_out = mixer_w
            proj = _mod_matmul(xf, norm_attn, mod3, 0, 1, seq, w_in.astype(BF16), *_TILE_IN)
            o = _dilated_layer(proj, slopes3, bsz, seq)
            xf = _matmul_resid(o, w_out.astype(BF16), xf, mod3, 2, seq, *_TILE_OUT)
        else:
            w_in, kv_norm, w_uk, w_uv, w_out = mixer_w
            proj = _mod_matmul(xf, norm_attn, mod3, 0, 1, seq, _dsa_in_weight(w_in), *_TILE_IN)
            o_lat = _dsa_layer(proj, kv_norm, w_uk, slopes3, bsz, seq, topk)
            xf = _heads_matmul_resid(o_lat, w_uv.astype(BF16), w_out.astype(BF16), xf, mod3, 2, seq, _TILE_OUT[0])
        act = _mod_matmul_swiglu(xf, norm_ffn, mod3, 3, 4, seq, w_gate_up.astype(BF16), *_TILE_GATE_UP)
        xf = _matmul_resid(act, w_down.astype(BF16), xf, mod3, 5, seq, *_TILE_DOWN)
    return _rmsnorm(xf, final_norm, x.dtype).reshape(bsz, seq, d)


def kernel(x, c, l0_norm_attn, l0_ada_w, l0_ada_b, l0_w_in, l0_w_out, l0_norm_ffn, l0_w_gate_up, l0_w_down, l1_norm_attn, l1_ada_w, l1_ada_b, l1_w_in, l1_kv_norm, l1_w_uk, l1_w_uv, l1_w_out, l1_norm_ffn, l1_w_gate_up, l1_w_down, l2_norm_attn, l2_ada_w, l2_ada_b, l2_w_in, l2_w_out, l2_norm_ffn, l2_w_gate_up, l2_w_down, l3_norm_attn, l3_ada_w, l3_ada_b, l3_w_in, l3_kv_norm, l3_w_uk, l3_w_uv, l3_w_out, l3_norm_ffn, l3_w_gate_up, l3_w_down, final_norm):
    layers = (
        (l0_norm_attn, l0_ada_w, l0_ada_b, (l0_w_in, l0_w_out), l0_norm_ffn, l0_w_gate_up, l0_w_down),
        (l1_norm_attn, l1_ada_w, l1_ada_b, (l1_w_in, l1_kv_norm, l1_w_uk, l1_w_uv, l1_w_out),
         l1_norm_ffn, l1_w_gate_up, l1_w_down),
        (l2_norm_attn, l2_ada_w, l2_ada_b, (l2_w_in, l2_w_out), l2_norm_ffn, l2_w_gate_up, l2_w_down),
        (l3_norm_attn, l3_ada_w, l3_ada_b, (l3_w_in, l3_kv_norm, l3_w_uk, l3_w_uv, l3_w_out),
         l3_norm_ffn, l3_w_gate_up, l3_w_down),
    )
    return _forward(x, c, layers, final_norm)
```

```python
import functools

import jax
import jax.numpy as jnp
from jax import lax
from jax.experimental import pallas as pl
from jax.experimental.pallas import tpu as pltpu

N_HEADS = 16
HEAD_DIM = 128
DILATED_BRANCHES = ((128, 1), (512, 4), (2048, 16))
KV_LATENT = 256
IDX_HEADS = 16
IDX_DIM = 128
TOPK_MAX = 256
EPS = 1e-6
NEG_BIG = -1e30
LOG2E = 1.4426950408889634

LANE = 128
BF16_ROWS = 16
VMEM_LIMIT = 56 * 1024 * 1024

F32 = jnp.float32
BF16 = jnp.bfloat16

_TILE_IN = (1024, 1536)
_TILE_GATE_UP = (1024, 512)
_TILE_OUT = (512, 2048)
_TILE_DOWN = (1024, 512)
_TILE_ROWS = 512
_DILATED_TQ = 256
_DSA_TQ = 128
_DSA_GROUP_ENDS = ((1, 8), (1, 4), (1, 2), (3, 4), (1, 1))
_DSA_HEADS_PER_DOT = 2

_DSA_COLS = {
    "q": 0,
    "c_kv": N_HEADS * HEAD_DIM,
    "q_idx": N_HEADS * HEAD_DIM + KV_LATENT,
    "k_idx": N_HEADS * HEAD_DIM + KV_LATENT + IDX_HEADS * IDX_DIM,
    "w_idx": N_HEADS * HEAD_DIM + KV_LATENT + IDX_HEADS * IDX_DIM + IDX_DIM,
}
assert _DSA_COLS["c_kv"] % KV_LATENT == 0 and all(v % LANE == 0 for v in _DSA_COLS.values())
_DSA_WIDTH = -(-(_DSA_COLS["w_idx"] + IDX_HEADS) // _TILE_IN[1]) * _TILE_IN[1]


def _params(*sem):
    return pltpu.CompilerParams(dimension_semantics=sem, vmem_limit_bytes=VMEM_LIMIT)


def _adaln_kernel(c_ref, w_ref, b_ref, o_ref):
    c = c_ref[...]
    sc = (c * jax.nn.sigmoid(c)).astype(BF16)
    acc = jnp.dot(sc, w_ref[...].astype(BF16), preferred_element_type=F32)
    o_ref[...] = acc + b_ref[...]


def _adaln(c, w, b, tn=1024):
    bsz, d = c.shape
    n = w.shape[1]
    return pl.pallas_call(
        _adaln_kernel,
        grid=(n // tn,),
        in_specs=[
            pl.BlockSpec((bsz, d), lambda j: (0, 0)),
            pl.BlockSpec((d, tn), lambda j: (0, j)),
            pl.BlockSpec((1, tn), lambda j: (0, j)),
        ],
        out_specs=pl.BlockSpec((bsz, tn), lambda j: (0, j)),
        out_shape=jax.ShapeDtypeStruct((bsz, n), F32),
        compiler_params=_params("parallel"),
        name="adaln",
    )(c, w, b.reshape(1, n))


def _modulate_chunk(h_ref, x_ref, g_ref, shift_ref, scale_ref, chunk_rows):
    i, j = pl.program_id(0), pl.program_id(1)
    tm = x_ref.shape[0]
    start = pl.multiple_of(jnp.minimum(j * chunk_rows, tm - chunk_rows), BF16_ROWS)
    rows = pl.ds(start, chunk_rows)
    x = x_ref[rows, :]
    ms = jnp.mean(x * x, axis=-1, keepdims=True)
    col_scale = g_ref[...] * (1.0 + scale_ref[0])
    h_ref[i % 2, rows, :] = (x * lax.rsqrt(ms + EPS) * col_scale + shift_ref[0]).astype(h_ref.dtype)


def _mod_mm_kernel(x_ref, g_ref, shift_ref, scale_ref, w_ref, o_ref, h_ref, *, chunk_rows):
    i = pl.program_id(0)

    @pl.when(i == 0)
    def _():
        _modulate_chunk(h_ref, x_ref, g_ref, shift_ref, scale_ref, chunk_rows)

    @pl.when(i > 0)
    def _():
        _modulate_chunk(h_ref, x_ref, g_ref, shift_ref, scale_ref, chunk_rows)
        o_ref[...] = jnp.dot(h_ref[(i - 1) % 2], w_ref[...], preferred_element_type=F32).astype(o_ref.dtype)


def _mod_mm_swiglu_kernel(x_ref, g_ref, shift_ref, scale_ref, wg_ref, wu_ref, o_ref, h_ref, *, chunk_rows):
    i = pl.program_id(0)

    @pl.when(i == 0)
    def _():
        _modulate_chunk(h_ref, x_ref, g_ref, shift_ref, scale_ref, chunk_rows)

    @pl.when(i > 0)
    def _():
        _modulate_chunk(h_ref, x_ref, g_ref, shift_ref, scale_ref, chunk_rows)
        h = h_ref[(i - 1) % 2]
        g = jnp.dot(h, wg_ref[...], preferred_element_type=F32)
        u = jnp.dot(h, wu_ref[...], preferred_element_type=F32)
        o_ref[...] = (g * jax.nn.sigmoid(g) * u).astype(o_ref.dtype)


def _mod_mm_call(kern, name, x, gain, mod3, shift_k, scale_k, seq, ws, w_col_offs, n, tm, tn):
    m, d = x.shape
    assert m % tm == 0 and n % tn == 0 and seq % tm == 0
    per_b = seq // tm
    n_i, n_j = m // tm, n // tn
    chunk_rows = -(-(-(-tm // n_j)) // BF16_ROWS) * BF16_ROWS
    assert chunk_rows <= tm

    def blk(i):
        return jnp.minimum(i, n_i - 1)

    def col(i, j):
        return jnp.where(i == 0, 0, j)

    w_specs = [pl.BlockSpec((d, tn), lambda i, j, off=off: (0, col(i, j) + off)) for off in w_col_offs]
    return pl.pallas_call(
        functools.partial(kern, chunk_rows=chunk_rows),
        grid=(n_i + 1, n_j),
        in_specs=[
            pl.BlockSpec((tm, d), lambda i, j: (blk(i), 0)),
            pl.BlockSpec((1, d), lambda i, j: (0, 0)),
            pl.BlockSpec((1, 1, d), lambda i, j: ((blk(i) // per_b) * 6 + shift_k, 0, 0)),
            pl.BlockSpec((1, 1, d), lambda i, j: ((blk(i) // per_b) * 6 + scale_k, 0, 0)),
        ] + w_specs,
        out_specs=pl.BlockSpec((tm, tn), lambda i, j: (jnp.maximum(i - 1, 0), col(i, j))),
        out_shape=jax.ShapeDtypeStruct((m, n), BF16),
        scratch_shapes=[pltpu.VMEM((2, tm, d), BF16)],
        compiler_params=_params("arbitrary", "arbitrary"),
        name=name,
    )(x, gain.reshape(1, d), mod3, mod3, *ws)


def _mod_matmul(x, gain, mod3, shift_k, scale_k, seq, w, tm, tn):
    return _mod_mm_call(_mod_mm_kernel, "mod_matmul", x, gain, mod3, shift_k, scale_k, seq, [w], [0],
                        w.shape[1], tm, tn)


def _mod_matmul_swiglu(x, gain, mod3, shift_k, scale_k, seq, w_gate_up, tm, tn):
    f = w_gate_up.shape[1] // 2
    return _mod_mm_call(_mod_mm_swiglu_kernel, "mod_matmul_swiglu", x, gain, mod3, shift_k, scale_k, seq,
                        [w_gate_up, w_gate_up], [0, f // tn], f, tm, tn)


def _rmsnorm_kernel(x_ref, g_ref, o_ref):
    x = x_ref[...].astype(F32)
    ms = jnp.mean(x * x, axis=-1, keepdims=True)
    o_ref[...] = (x * lax.rsqrt(ms + EPS) * g_ref[...]).astype(o_ref.dtype)


def _rmsnorm(x, gain, out_dtype, tm=_TILE_ROWS):
    m, d = x.shape
    return pl.pallas_call(
        _rmsnorm_kernel,
        grid=(m // tm,),
        in_specs=[
            pl.BlockSpec((tm, d), lambda i: (i, 0)),
            pl.BlockSpec((1, d), lambda i: (0, 0)),
        ],
        out_specs=pl.BlockSpec((tm, d), lambda i: (i, 0)),
        out_shape=jax.ShapeDtypeStruct((m, d), out_dtype),
        compiler_params=_params("parallel"),
        name="rmsnorm",
    )(x, gain.reshape(1, d))


def _mm_resid_kernel(a_ref, w_ref, x_ref, gate_ref, o_ref):
    acc = jnp.dot(a_ref[...], w_ref[...], preferred_element_type=F32)
    o_ref[...] = x_ref[...] + gate_ref[0] * acc


def _matmul_resid(a, w, x, mod3, gate_k, seq, tm, tn):
    m, kd = a.shape
    n = w.shape[1]
    assert m % tm == 0 and n % tn == 0 and seq % tm == 0
    per_b = seq // tm
    return pl.pallas_call(
        _mm_resid_kernel,
        grid=(m // tm, n // tn),
        in_specs=[
            pl.BlockSpec((tm, kd), lambda i, j: (i, 0)),
            pl.BlockSpec((kd, tn), lambda i, j: (0, j)),
            pl.BlockSpec((tm, tn), lambda i, j: (i, j)),
            pl.BlockSpec((1, 1, tn), lambda i, j: ((i // per_b) * 6 + gate_k, 0, j)),
        ],
        out_specs=pl.BlockSpec((tm, tn), lambda i, j: (i, j)),
        out_shape=jax.ShapeDtypeStruct((m, n), F32),
        compiler_params=_params("parallel", "parallel"),
        name="matmul_resid",
    )(a, w, x, mod3)


def _head_project(a_ref, w_ref, scale=1.0):
    n_heads, din, dout = w_ref.shape
    parts = []
    for h in range(n_heads):
        acc = jnp.dot(a_ref[:, h * din:(h + 1) * din], w_ref[h], preferred_element_type=F32)
        parts.append((acc * scale).astype(BF16) if scale != 1.0 else acc.astype(BF16))
    return jnp.concatenate(parts, axis=1)


def _heads_mm_resid_kernel(a_ref, wh_ref, w_ref, x_ref, gate_ref, o_ref):
    acc = jnp.dot(_head_project(a_ref, wh_ref), w_ref[...], preferred_element_type=F32)
    o_ref[...] = x_ref[...] + gate_ref[0] * acc


def _heads_matmul_resid(a, w_heads, w, x, mod3, gate_k, seq, tm):
    m = a.shape[0]
    kd, n = w.shape
    assert m % tm == 0 and seq % tm == 0 and w_heads.shape[0] * w_heads.shape[2] == kd
    per_b = seq // tm
    return pl.pallas_call(
        _heads_mm_resid_kernel,
        grid=(m // tm,),
        in_specs=[
            pl.BlockSpec((tm, a.shape[1]), lambda i: (i, 0)),
            pl.BlockSpec(w_heads.shape, lambda i: (0, 0, 0)),
            pl.BlockSpec((kd, n), lambda i: (0, 0)),
            pl.BlockSpec((tm, n), lambda i: (i, 0)),
            pl.BlockSpec((1, 1, n), lambda i: ((i // per_b) * 6 + gate_k, 0, 0)),
        ],
        out_specs=pl.BlockSpec((tm, n), lambda i: (i, 0)),
        out_shape=jax.ShapeDtypeStruct((m, n), F32),
        compiler_params=_params("parallel"),
        name="heads_matmul_resid",
    )(a, w_heads, w, x, mod3)


def _qlat_ckv_kernel(q_ref, ckv_ref, w_ref, g_ref, qlat_ref, ckvn_ref, *, scale):
    qlat_ref[...] = _head_project(q_ref, w_ref, scale)
    c = ckv_ref[...].astype(F32)
    ms = jnp.mean(c * c, axis=-1, keepdims=True)
    ckvn_ref[...] = (c * lax.rsqrt(ms + EPS) * g_ref[...]).astype(ckvn_ref.dtype)


def _qlat_and_ckv(proj, w_uk, kv_norm, scale, tm=2 * _TILE_ROWS):
    m = proj.shape[0]
    h, din, dout = w_uk.shape
    c = kv_norm.shape[0]
    return pl.pallas_call(
        functools.partial(_qlat_ckv_kernel, scale=scale),
        grid=(m // tm,),
        in_specs=[
            pl.BlockSpec((tm, h * din), lambda i: (i, _DSA_COLS["q"] // (h * din))),
            pl.BlockSpec((tm, c), lambda i: (i, _DSA_COLS["c_kv"] // c)),
            pl.BlockSpec((h, din, dout), lambda i: (0, 0, 0)),
            pl.BlockSpec((1, c), lambda i: (0, 0)),
        ],
        out_specs=[
            pl.BlockSpec((tm, h * dout), lambda i: (i, 0)),
            pl.BlockSpec((tm, c), lambda i: (i, 0)),
        ],
        out_shape=[jax.ShapeDtypeStruct((m, h * dout), BF16), jax.ShapeDtypeStruct((m, c), BF16)],
        compiler_params=_params("parallel"),
        name="qlat_ckv",
    )(proj, proj, w_uk, kv_norm.reshape(1, c))


def _nt_dot(a, b):
    return lax.dot_general(a, b, (((1,), (1,)), ((), ())), preferred_element_type=F32)


def _dilated_kernel(slope_ref, q_ref, k_ref, v_ref, o_ref, bias_ref, vaug_ref, *, seq, branches, tq):
    n_tiles = seq // tq

    @pl.when(pl.program_id(1) == 0)
    def _():
        slope = slope_ref[0][:, :1]
        qi = lax.broadcasted_iota(jnp.int32, (tq, seq), 0)
        kj = lax.broadcasted_iota(jnp.int32, (tq, seq), 1)
        d = qi + (seq - tq) - kj
        count = jnp.zeros((tq, seq), F32)
        for window, dil in branches:
            assert dil & (dil - 1) == 0
            count = count + jnp.where(((d & (dil - 1)) == 0) & (d <= window), 1.0, 0.0)
        bias = jnp.log2(jnp.maximum(count, 1.0)) - (slope * LOG2E) * d.astype(F32)
        bias_ref[...] = jnp.where((d >= 0) & (count > 0.0), bias, NEG_BIG)
        vaug_ref[:, HEAD_DIM:] = jnp.ones((seq, LANE), BF16)

    vaug_ref[:, :HEAD_DIM] = v_ref[0]

    for n in range(n_tiles):
        n_keys = (n + 1) * tq
        q = (q_ref[0, n * tq:(n + 1) * tq, :].astype(F32) * (HEAD_DIM ** -0.5 * LOG2E)).astype(BF16)
        s = _nt_dot(q, k_ref[0, :n_keys, :]) + bias_ref[:, seq - n_keys:]
        m = jnp.max(s, axis=-1, keepdims=True)
        p = jnp.exp2(s - m).astype(BF16)
        pv = jnp.dot(p, vaug_ref[:n_keys, :], preferred_element_type=F32)
        o_ref[0, n * tq:(n + 1) * tq, :] = (pv[:, :HEAD_DIM] / pv[:, HEAD_DIM:]).astype(o_ref.dtype)


def _dilated_attention(qkv, slopes3, bsz, seq, tq=_DILATED_TQ):
    h = N_HEADS
    assert seq % tq == 0
    kern = functools.partial(_dilated_kernel, seq=seq, branches=DILATED_BRANCHES, tq=tq)
    return pl.pallas_call(
        kern,
        grid=(h, bsz),
        in_specs=[
            pl.BlockSpec((1, 1, LANE), lambda j, b: (j, 0, 0)),
            pl.BlockSpec((1, seq, HEAD_DIM), lambda j, b: (b, 0, j)),
            pl.BlockSpec((1, seq, HEAD_DIM), lambda j, b: (b, 0, h + j)),
            pl.BlockSpec((1, seq, HEAD_DIM), lambda j, b: (b, 0, 2 * h + j)),
        ],
        out_specs=pl.BlockSpec((1, seq, HEAD_DIM), lambda j, b: (b, 0, j)),
        out_shape=jax.ShapeDtypeStruct((bsz, seq, h * HEAD_DIM), BF16),
        scratch_shapes=[
            pltpu.VMEM((tq, seq), F32),
            pltpu.VMEM((seq, HEAD_DIM + LANE), BF16),
        ],
        compiler_params=_params("parallel", "arbitrary"),
        name="dilated_attention",
    )(slopes3, qkv, qkv, qkv)


def _stack_heads(ref, col0, width, heads):
    parts = [ref[0, :, col0 + h * width:col0 + (h + 1) * width] for h in heads]
    return parts[0] if len(parts) == 1 else jnp.concatenate(parts, axis=0)


def _dsa_kernel(slope_ref, rows_ref, kidx_ref, qlat_ref, ckv_ref, o_ref, *, seq, topk, tq, group_ends, **static):
    t = pl.program_id(1)
    start = 0
    for end in group_ends:
        body = functools.partial(_dsa_body, slope_ref, rows_ref, kidx_ref, qlat_ref, ckv_ref, o_ref,
                                 seq=end * tq, topk=topk, tq=tq, **static)
        pl.when((t >= start) & (t < end))(body)
        start = end


def _dsa_body(slope_ref, rows_ref, kidx_ref, qlat_ref, ckv_ref, o_ref, *, seq, topk, tq, qidx_col, widx_col,
              heads_per_dot):
    t = pl.program_id(1)
    q_pos = t * tq + lax.broadcasted_iota(jnp.int32, (tq, seq), 0)
    k_pos = lax.broadcasted_iota(jnp.int32, (tq, seq), 1)
    causal = k_pos <= q_pos

    kidx = kidx_ref[0, :seq, :]
    w_all = rows_ref[0, :, widx_col:widx_col + LANE].astype(F32) * (IDX_DIM ** -0.5 * IDX_HEADS ** -0.5)
    score = jnp.zeros((tq, seq), F32)
    for h0 in range(0, IDX_HEADS, heads_per_dot):
        heads = range(h0, h0 + heads_per_dot)
        logits = _nt_dot(_stack_heads(rows_ref, qidx_col, IDX_DIM, heads), kidx)
        for j, h in enumerate(heads):
            score = score + w_all[:, h:h + 1] * jnp.maximum(logits[j * tq:(j + 1) * tq], 0.0)
    score = jnp.where(causal, score, NEG_BIG)
    score = jnp.where(score == 0.0, 0.0, score)

    bits = lax.bitcast_convert_type(score, jnp.int32)
    keys = jnp.where(bits < 0, bits ^ jnp.int32(0x7FFFFFFF), bits)

    kf = float(topk)
    int_min = jnp.int32(-(2 ** 31))
    keys_t = keys.T
    n_acc = 8

    def count_ge_t(cand):
        hit = jnp.where(keys_t >= cand, 1.0, 0.0).reshape(n_acc, seq // (8 * n_acc), 8, tq)
        partial = jnp.sum(jnp.sum(hit, axis=1), axis=0)
        return jnp.sum(partial, axis=0, keepdims=True)

    n_zero = count_ge_t(jnp.int32(0))
    thr_t = jnp.where(n_zero >= kf, jnp.int32(0), int_min)
    n_ge_t = jnp.where(n_zero >= kf, n_zero, float(seq))

    def bit_step(i, carry):
        thr_t, n_ge_t = carry
        cand = thr_t | (jnp.int32(1) << (30 - i))
        n_cand = count_ge_t(cand)
        keep = n_cand >= kf
        return jnp.where(keep, cand, thr_t), jnp.where(keep, n_cand, n_ge_t)

    all_kept = (t + 1) * tq <= topk
    thr_t, n_ge_t = lax.fori_loop(0, jnp.where(all_kept, 0, 31), bit_step, (thr_t, n_ge_t))
    thr_t = jnp.where(all_kept, int_min, thr_t)
    thr = jnp.broadcast_to(thr_t, (tq, tq)).T[:, :1]

    def exact_k():
        return jnp.where((keys >= thr) & causal, 0.0, NEG_BIG)

    def with_ties():
        gt = keys > thr
        eq = keys == thr
        n_gt = jnp.sum(jnp.where(gt, 1.0, 0.0), axis=-1, keepdims=True)
        need = kf - n_gt

        def cut_step(i, cut):
            cand = cut - (jnp.int32(1) << (n_bits - 1 - i))
            cnt = jnp.sum(jnp.where(eq & (k_pos <= cand), 1.0, 0.0), axis=-1, keepdims=True)
            return jnp.where(cnt >= need, cand, cut)

        n_bits = (seq - 1).bit_length()
        cut = lax.fori_loop(0, n_bits, cut_step, jnp.full((tq, 1), (1 << n_bits) - 1, jnp.int32))
        return jnp.where((gt | (eq & (k_pos <= cut))) & causal, 0.0, NEG_BIG)

    mask_bias = lax.cond(jnp.logical_and(jnp.max(n_ge_t) > kf, jnp.logical_not(all_kept)), with_ties, exact_k)

    k_pos_row = lax.broadcasted_iota(jnp.int32, (1, seq), 1).astype(F32)
    ckv = ckv_ref[0, :seq, :]
    for h0 in range(0, N_HEADS, heads_per_dot):
        heads = range(h0, h0 + heads_per_dot)
        s_all = _nt_dot(_stack_heads(qlat_ref, 0, KV_LATENT, heads), ckv)
        probs, sums = [], []
        for j, h in enumerate(heads):
            slope = slope_ref[h][:, :1]
            s = s_all[j * tq:(j + 1) * tq] + (slope * LOG2E) * k_pos_row + mask_bias
            p = jnp.exp2(s - jnp.max(s, axis=-1, keepdims=True))
            sums.append(jnp.sum(p, axis=-1, keepdims=True))
            probs.append(p.astype(BF16))
        p_all = probs[0] if len(probs) == 1 else jnp.concatenate(probs, axis=0)
        o_all = jnp.dot(p_all, ckv, preferred_element_type=F32)
        for j, h in enumerate(heads):
            o = o_all[j * tq:(j + 1) * tq] / sums[j]
            o_ref[0, :, h * KV_LATENT:(h + 1) * KV_LATENT] = o.astype(o_ref.dtype)


def _dsa_attention(proj, qlat, ckv, slopes3, bsz, seq, topk):
    h = N_HEADS
    width = proj.shape[-1]
    tq = _DSA_TQ
    n_tiles = seq // tq
    assert all(n_tiles * num % den == 0 for num, den in _DSA_GROUP_ENDS) and N_HEADS % _DSA_HEADS_PER_DOT == 0
    group_ends = tuple(n_tiles * num // den for num, den in _DSA_GROUP_ENDS)
    kern = functools.partial(_dsa_kernel, seq=seq, topk=topk, tq=tq, group_ends=group_ends,
                             qidx_col=_DSA_COLS["q_idx"], widx_col=_DSA_COLS["w_idx"],
                             heads_per_dot=_DSA_HEADS_PER_DOT)
    return pl.pallas_call(
        kern,
        grid=(bsz, seq // tq),
        in_specs=[
            pl.BlockSpec((h, 1, LANE), lambda b, t: (0, 0, 0)),
            pl.BlockSpec((1, tq, width), lambda b, t: (b, t, 0)),
            pl.BlockSpec((1, seq, IDX_DIM), lambda b, t: (b, 0, _DSA_COLS["k_idx"] // IDX_DIM)),
            pl.BlockSpec((1, tq, h * KV_LATENT), lambda b, t: (b, t, 0)),
            pl.BlockSpec((1, seq, KV_LATENT), lambda b, t: (b, 0, 0)),
        ],
        out_specs=pl.BlockSpec((1, tq, h * KV_LATENT), lambda b, t: (b, t, 0)),
        out_shape=jax.ShapeDtypeStruct((bsz, seq, h * KV_LATENT), BF16),
        compiler_params=_params("parallel", "parallel"),
        name="dsa_attention",
    )(slopes3, proj, proj, qlat, ckv)


def _dsa_in_weight(w_in):
    return jnp.pad(w_in.astype(BF16), ((0, 0), (0, _DSA_WIDTH - w_in.shape[1])))


def _dilated_layer(proj, slopes3, bsz, seq):
    return _dilated_attention(proj.reshape(bsz, seq, -1), slopes3, bsz, seq).reshape(bsz * seq, -1)


def _dsa_layer(proj, kv_norm, w_uk, slopes3, bsz, seq, topk):
    qlat, ckv = _qlat_and_ckv(proj, w_uk.astype(BF16), kv_norm, HEAD_DIM ** -0.5 * LOG2E)
    o_lat = _dsa_attention(
        proj.reshape(bsz, seq, -1), qlat.reshape(bsz, seq, -1), ckv.reshape(bsz, seq, -1), slopes3, bsz, seq, topk
    )
    return o_lat.reshape(bsz * seq, -1)


def _forward(x, c, layers, final_norm):
    bsz, seq, d = x.shape
    m = bsz * seq
    topk = min(TOPK_MAX, seq // 4)
    slopes = jnp.exp2(-8.0 * jnp.arange(1, N_HEADS + 1, dtype=F32) / N_HEADS)
    slopes3 = jnp.broadcast_to(slopes[:, None, None], (N_HEADS, 1, LANE))
    xf = x.reshape(m, d)
    for i, (norm_attn, ada_w, ada_b, mixer_w, norm_ffn, w_gate_up, w_down) in enumerate(layers):
        mod3 = _adaln(c, ada_w, ada_b).reshape(bsz * 6, 1, d)
        if i % 2 == 0:
            w_in, w_out = mixer_w
            proj = _mod_matmul(xf, norm_attn, mod3, 0, 1, seq, w_in.astype(BF16), *_TILE_IN)
            o = _dilated_layer(proj, slopes3, bsz, seq)
            xf = _matmul_resid(o, w_out.astype(BF16), xf, mod3, 2, seq, *_TILE_OUT)
        else:
            w_in, kv_norm, w_uk, w_uv, w_out = mixer_w
            proj = _mod_matmul(xf, norm_attn, mod3, 0, 1, seq, _dsa_in_weight(w_in), *_TILE_IN)
            o_lat = _dsa_layer(proj, kv_norm, w_uk, slopes3, bsz, seq, topk)
            xf = _heads_matmul_resid(o_lat, w_uv.astype(BF16), w_out.astype(BF16), xf, mod3, 2, seq, _TILE_OUT[0])
        act = _mod_matmul_swiglu(xf, norm_ffn, mod3, 3, 4, seq, w_gate_up.astype(BF16), *_TILE_GATE_UP)
        xf = _matmul_resid(act, w_down.astype(BF16), xf, mod3, 5, seq, *_TILE_DOWN)
    return _rmsnorm(xf, final_norm, x.dtype).reshape(bsz, seq, d)


def kernel(x, c, l0_norm_attn, l0_ada_w, l0_ada_b, l0_w_in, l0_w_out, l0_norm_ffn, l0_w_gate_up, l0_w_down, l1_norm_attn, l1_ada_w, l1_ada_b, l1_w_in, l1_kv_norm, l1_w_uk, l1_w_uv, l1_w_out, l1_norm_ffn, l1_w_gate_up, l1_w_down, l2_norm_attn, l2_ada_w, l2_ada_b, l2_w_in, l2_w_out, l2_norm_ffn, l2_w_gate_up, l2_w_down, l3_norm_attn, l3_ada_w, l3_ada_b, l3_w_in, l3_kv_norm, l3_w_uk, l3_w_uv, l3_w_out, l3_norm_ffn, l3_w_gate_up, l3_w_down, final_norm):
    layers = (
        (l0_norm_attn, l0_ada_w, l0_ada_b, (l0_w_in, l0_w_out), l0_norm_ffn, l0_w_gate_up, l0_w_down),
        (l1_norm_attn, l1_ada_w, l1_ada_b, (l1_w_in, l1_kv_norm, l1_w_uk, l1_w_uv, l1_w_out),
         l1_norm_ffn, l1_w_gate_up, l1_w_down),
        (l2_norm_attn, l2_ada_w, l2_ada_b, (l2_w_in, l2_w_out), l2_norm_ffn, l2_w_gate_up, l2_w_down),
        (l3_norm_attn, l3_ada_w, l3_ada_b, (l3_w_in, l3_kv_norm, l3_w_uk, l3_w_uv, l3_w_out),
         l3_norm_ffn, l3_w_gate_up, l3_w_down),
    )
    return _forward(x, c, layers, final_norm)
```
